```python
import jax, jax.numpy as jnp
from jax import lax
import numpy as np

D_MODEL = 1024
BATCH = 32
SEQ = 2048
DEPTH = 1

PLE_DIM = 256
EPS = 1e-6

SSD_HEADDIM = 64
SSD_INNER = D_MODEL
SSD_HEADS = SSD_INNER // SSD_HEADDIM
SSD_GROUPS = 2
SSD_STATE = 128
SSD_CONV = 4
SSD_CHUNK = 128
SSD_CONV_CH = SSD_INNER + 2 * SSD_GROUPS * SSD_STATE

CONF_CH = D_MODEL
CONF_KERNEL = 31

OFF_XBC = SSD_INNER
OFF_DT = OFF_XBC + SSD_CONV_CH
OFF_GLU = OFF_DT + SSD_HEADS
OFF_GATE = OFF_GLU + 2 * CONF_CH
N_IN = OFF_GATE + 2 * D_MODEL

N_GROUPS = 4
EXPERTS_PER_GROUP = 8
N_EXPERTS = N_GROUPS * EXPERTS_PER_GROUP
TOP_K = 2
D_EXPERT = 512
MOE_BLOCK = 256

kernel_name = 'hybrid_ssd_conformer_hmoe_block'


def rms_norm(x, g):
    xf = x.astype(jnp.float32)
    y = xf * lax.rsqrt(jnp.mean(xf * xf, axis=-1, keepdims=True) + EPS)
    return (y * g.astype(jnp.float32)).astype(x.dtype)


def layer_norm(x, g, b):
    xf = x.astype(jnp.float32)
    mu = jnp.mean(xf, axis=-1, keepdims=True)
    xc = xf - mu
    y = xc * lax.rsqrt(jnp.mean(xc * xc, axis=-1, keepdims=True) + EPS)
    return (y * g.astype(jnp.float32) + b.astype(jnp.float32)).astype(x.dtype)


def causal_dwconv(x, w, b):
    k, c = w.shape
    y = lax.conv_general_dilated(
        x, w[:, None, :].astype(x.dtype), window_strides=(1,), padding=((k - 1, 0),),
        dimension_numbers=('NWC', 'WIO', 'NWC'), feature_group_count=c)
    return y + b.astype(x.dtype)


def ssd_chunked(x, dt, a, b_mat, c_mat):
    bsz, seq, h, p = x.shape
    g, n = b_mat.shape[2], b_mat.shape[3]
    r = h // g
    q = SSD_CHUNK
    nc = seq // q
    xdt = (x * dt[..., None]).reshape(bsz, nc, q, g, r, p)
    a_cs = jnp.cumsum((dt * a).reshape(bsz, nc, q, g, r), axis=2)
    bc = b_mat.reshape(bsz, nc, q, g, n)
    cc = c_mat.reshape(bsz, nc, q, g, n)
    causal = jnp.tril(jnp.ones((q, q), dtype=bool))[None, None, :, :, None, None]
    seg = a_cs[:, :, :, None] - a_cs[:, :, None, :]
    decay = jnp.exp(jnp.where(causal, seg, -jnp.inf))
    cb = jnp.einsum('bclgn,bcsgn->bclsg', cc, bc)
    y_diag = jnp.einsum('bclsgr,bcsgrp->bclgrp', cb[..., None] * decay, xdt)
    decay_end = jnp.exp(a_cs[:, :, -1:] - a_cs)
    states = jnp.einsum('bcsgn,bcsgrp->bcgrpn', bc, xdt * decay_end[..., None])
    chunk_decay = jnp.exp(a_cs[:, :, -1])

    def step(carry, inp):
        st, dec = inp
        return carry * dec[..., None, None] + st, carry

    init = jnp.zeros((bsz, g, r, p, n), jnp.float32)
    _, prev = lax.scan(step, init, (jnp.moveaxis(states, 1, 0), jnp.moveaxis(chunk_decay, 1, 0)))
    prev = jnp.moveaxis(prev, 0, 1)
    y_off = jnp.einsum('bclgn,bcgrpn->bclgrp', cc, prev) * jnp.exp(a_cs)[..., None]
    return (y_diag + y_off).reshape(bsz, seq, h, p)


def ssd_branch(proj, conv_w, conv_b, dt_bias, a_log, d_skip, norm_g, w_out):
    bsz, seq, _ = proj.shape
    z = proj[..., :SSD_INNER]
    xbc = jax.nn.silu(causal_dwconv(proj[..., OFF_XBC:OFF_DT], conv_w, conv_b))
    dt_raw = proj[..., OFF_DT:OFF_GLU]
    gn = SSD_GROUPS * SSD_STATE
    xs = xbc[..., :SSD_INNER].astype(jnp.float32).reshape(bsz, seq, SSD_HEADS, SSD_HEADDIM)
    bm = xbc[..., SSD_INNER:SSD_INNER + gn].astype(jnp.float32).reshape(bsz, seq, SSD_GROUPS, SSD_STATE)
    cm = xbc[..., SSD_INNER + gn:].astype(jnp.float32).reshape(bsz, seq, SSD_GROUPS, SSD_STATE)
    dt = jax.nn.softplus(dt_raw.astype(jnp.float32) + dt_bias.astype(jnp.float32))
    a = -jnp.exp(a_log.astype(jnp.float32))
    y = ssd_chunked(xs, dt, a, bm, cm) + d_skip.astype(jnp.float32)[:, None] * xs
    y = y.reshape(bsz, seq, SSD_INNER).astype(proj.dtype)
    y = rms_norm(y * jax.nn.silu(z), norm_g)
    return y @ w_out


def conformer_branch(proj, dw_w, dw_b, ln_g, ln_b, w_out):
    glu = proj[..., OFF_GLU:OFF_GLU + CONF_CH] * jax.nn.sigmoid(proj[..., OFF_GLU + CONF_CH:OFF_GATE])
    c = causal_dwconv(glu, dw_w, dw_b)
    c = jax.nn.silu(layer_norm(c, ln_g, ln_b))
    return c @ w_out


def hier_moe(u, wg, bg, we, be, w_gate, w_up, w_down):
    n, d = u.shape
    g_logits = (u @ wg).astype(jnp.float32) + bg.astype(jnp.float32)
    g_idx = jnp.argmax(g_logits, axis=-1)
    g_prob = jnp.take_along_axis(jax.nn.softmax(g_logits, axis=-1), g_idx[:, None], axis=-1)
    e_logits = ((u @ we).astype(jnp.float32) + be.astype(jnp.float32)).reshape(n, N_GROUPS, EXPERTS_PER_GROUP)
    e_in = jnp.take_along_axis(e_logits, g_idx[:, None, None], axis=1)[:, 0]
    top_v, top_i = lax.top_k(e_in, TOP_K)
    gate_w = jax.nn.softmax(top_v, axis=-1) * g_prob
    expert = g_idx[:, None] * EXPERTS_PER_GROUP + top_i
    n_assign = n * TOP_K
    flat_e = expert.reshape(n_assign)
    order = jnp.argsort(flat_e)
    sorted_e = flat_e[order]
    sorted_tok = order // TOP_K
    sorted_w = gate_w.reshape(n_assign)[order].astype(u.dtype)
    counts = jnp.bincount(flat_e, length=N_EXPERTS)
    padded = (counts + MOE_BLOCK - 1) // MOE_BLOCK * MOE_BLOCK
    pad_end = jnp.cumsum(padded)
    pad_start = pad_end - padded
    start = jnp.cumsum(counts) - counts
    dest = pad_start[sorted_e] + jnp.arange(n_assign) - start[sorted_e]
    n_blocks = -(-n_assign // MOE_BLOCK) + N_EXPERTS
    rows = n_blocks * MOE_BLOCK
    buf = jnp.zeros((rows, d), u.dtype).at[dest].set(u[sorted_tok])
    blk_e = jnp.minimum(jnp.searchsorted(pad_end, jnp.arange(n_blocks) * MOE_BLOCK, side='right'), N_EXPERTS - 1)

    def expert_block(args):
        xb, e = args
        hdn = jax.nn.silu(xb @ w_gate[e]) * (xb @ w_up[e])
        return hdn @ w_down[e]

    yb = lax.map(expert_block, (buf.reshape(n_blocks, MOE_BLOCK, d), blk_e)).reshape(rows, d)
    contrib = yb[dest] * sorted_w[:, None]
    return jnp.zeros((n, d), u.dtype).at[sorted_tok].add(contrib)


def setup_inputs(seed: int = 0) -> dict:
    key = jax.random.key(seed)
    ks = iter(jax.random.split(key, 40))
    f32 = jnp.float32

    def nrm(shape, scale):
        return jax.random.normal(next(ks), shape, f32) * scale

    L = DEPTH
    dt0 = jnp.exp(jax.random.uniform(next(ks), (L, SSD_HEADS), f32, np.log(1e-3), np.log(1e-1)))
    dt_bias = dt0 + jnp.log(-jnp.expm1(-dt0))
    a_log = jnp.log(jax.random.uniform(next(ks), (L, SSD_HEADS), f32, 1.0, 16.0))
    return {
        'x': nrm((BATCH, SEQ, D_MODEL), 1.0),
        'p': nrm((DEPTH, BATCH, SEQ, PLE_DIM), 1.0),
        'norm_mix_g': 1.0 + nrm((L, D_MODEL), 0.01),
        'w_in': nrm((L, D_MODEL, N_IN), D_MODEL ** -0.5),
        'ssd_conv_w': nrm((L, SSD_CONV, SSD_CONV_CH), SSD_CONV ** -0.5),
        'ssd_conv_b': nrm((L, SSD_CONV_CH), 0.01),
        'ssd_dt_bias': dt_bias,
        'ssd_a_log': a_log,
        'ssd_d': 1.0 + nrm((L, SSD_HEADS), 0.1),
        'ssd_norm_g': 1.0 + nrm((L, SSD_INNER), 0.01),
        'w_ssd_out': nrm((L, SSD_INNER, D_MODEL), SSD_INNER ** -0.5),
        'conf_dw_w': nrm((L, CONF_KERNEL, CONF_CH), CONF_KERNEL ** -0.5),
        'conf_dw_b': nrm((L, CONF_CH), 0.01),
        'conf_ln_g': 1.0 + nrm((L, CONF_CH), 0.01),
        'conf_ln_b': nrm((L, CONF_CH), 0.01),
        'w_conf_out': nrm((L, CONF_CH, D_MODEL), CONF_CH ** -0.5),
        'w_o': nrm((L, D_MODEL, D_MODEL), D_MODEL ** -0.5),
        'norm_ffn_g': 1.0 + nrm((L, D_MODEL), 0.01),
        'router_group_w': nrm((L, D_MODEL, N_GROUPS), D_MODEL ** -0.5),
        'router_group_b': nrm((L, N_GROUPS), 0.01),
        'router_expert_w': nrm((L, D_MODEL, N_EXPERTS), D_MODEL ** -0.5),
        'router_expert_b': nrm((L, N_EXPERTS), 0.01),
        'expert_w_gate': nrm((L, N_EXPERTS, D_MODEL, D_EXPERT), D_MODEL ** -0.5),
        'expert_w_up': nrm((L, N_EXPERTS, D_MODEL, D_EXPERT), D_MODEL ** -0.5),
        'expert_w_down': nrm((L, N_EXPERTS, D_EXPERT, D_MODEL), D_EXPERT ** -0.5),
        'norm_ple_g': 1.0 + nrm((L, D_MODEL), 0.01),
        'w_ple_gate': nrm((L, D_MODEL, D_MODEL), D_MODEL ** -0.5),
        'w_ple_proj': nrm((L, PLE_DIM, D_MODEL), PLE_DIM ** -0.5),
        'final_norm_g': 1.0 + nrm((D_MODEL,), 0.01),
    }


def reference(x, p, norm_mix_g, w_in, ssd_conv_w, ssd_conv_b, ssd_dt_bias, ssd_a_log, ssd_d,
              ssd_norm_g, w_ssd_out, conf_dw_w, conf_dw_b, conf_ln_g, conf_ln_b, w_conf_out, w_o,
              norm_ffn_g, router_group_w, router_group_b, router_expert_w, router_expert_b,
              expert_w_gate, expert_w_up, expert_w_down, norm_ple_g, w_ple_gate, w_ple_proj,
              final_norm_g):
    bsz, seq, d = x.shape
    h = x
    for i in range(DEPTH):
        u = rms_norm(h, norm_mix_g[i])
        proj = u @ w_in[i]
        y_ssd = ssd_branch(proj, ssd_conv_w[i], ssd_conv_b[i], ssd_dt_bias[i], ssd_a_log[i],
                           ssd_d[i], ssd_norm_g[i], w_ssd_out[i])
        y_conv = conformer_branch(proj, conf_dw_w[i], conf_dw_b[i], conf_ln_g[i], conf_ln_b[i],
                                  w_conf_out[i])
        gate_ssd = jax.nn.sigmoid(proj[..., OFF_GATE:OFF_GATE + D_MODEL])
        gate_conv = jax.nn.sigmoid(proj[..., OFF_GATE + D_MODEL:])
        h = h + (gate_ssd * y_ssd + gate_conv * y_conv) @ w_o[i]
        u = rms_norm(h, norm_ffn_g[i]).reshape(bsz * seq, d)
        moe = hier_moe(u, router_group_w[i], router_group_b[i], router_expert_w[i],
                       router_expert_b[i], expert_w_gate[i], expert_w_up[i], expert_w_down[i])
        h = h + moe.reshape(bsz, seq, d)
        ple_gate = jax.nn.sigmoid(rms_norm(h, norm_ple_g[i]) @ w_ple_gate[i])
        h = h + (p[i] @ w_ple_proj[i]) * ple_gate
    return rms_norm(h, final_norm_g)
```

```python
import functools

import jax
import jax.numpy as jnp
from jax import lax
from jax.experimental import pallas as pl
from jax.experimental.pallas import tpu as pltpu

F32 = jnp.float32
BF16 = jnp.bfloat16
EPS = 1e-6

D_MODEL = 1024
PLE_DIM = 256
SSD_HEADDIM = 64
SSD_HEADS = 16
SSD_GROUPS = 2
SSD_STATE = 128
SSD_CONV = 4
SSD_CHUNK = 128
SSD_BC = SSD_GROUPS * SSD_STATE
SSD_CONV_CH = D_MODEL + 2 * SSD_BC
CONF_KERNEL = 31
N_GROUPS = 4
EXPERTS_PER_GROUP = 8
N_EXPERTS = 32
D_EXPERT = 512

OFF_XBC = D_MODEL
OFF_DT = OFF_XBC + SSD_CONV_CH
OFF_GLU = OFF_DT + SSD_HEADS
OFF_GATE = OFF_GLU + 2 * D_MODEL

LANES = 128
SUBLANES = 8
VMEM_LIMIT_BYTES = 56 * 1024 * 1024

ROW_TILE = 512
MOE_BLOCK_ROWS = 512
NEG_BIG = -1e30
HEADS_PER_SLAB = LANES // SSD_HEADDIM
N_SLABS = D_MODEL // LANES
CONF_HALO = 32


def _sigmoid(v):
    return jax.nn.sigmoid(v)


def _silu(v):
    return v * jax.nn.sigmoid(v)


def _rms(v, g):
    return v * lax.rsqrt(jnp.mean(v * v, axis=-1, keepdims=True) + EPS) * g


def _dot(a, b):
    return jnp.dot(a, b, preferred_element_type=F32)


def _split_bf16(v, parts):
    out = []
    r = v
    for _ in range(parts):
        h = r.astype(BF16)
        out.append(h)
        r = r - h.astype(F32)
    return out


def _const_spec(shape):
    nd = len(shape)
    return pl.BlockSpec(shape, lambda *_: (0,) * nd, pipeline_mode=pl.Buffered(1))


def _params(sem):
    return pltpu.CompilerParams(dimension_semantics=sem, vmem_limit_bytes=VMEM_LIMIT_BYTES)


def _inproj_kernel(x_ref, g_ref, w_ref, wdt_ref, dtb_ref,
                   z_ref, xbc_ref, dt_ref, glu_ref, gs_ref, gc_ref, u_scr):
    u_scr[...] = _rms(x_ref[...], g_ref[...]).astype(BF16)

    def seg(lo, n):
        return _dot(u_scr[...], w_ref[:, lo:lo + n])

    z_ref[...] = seg(0, D_MODEL).astype(BF16)
    blk = 512
    for j in range(SSD_CONV_CH // blk):
        xbc_ref[:, j * blk:(j + 1) * blk] = seg(D_MODEL + j * blk, blk).astype(BF16)
    o = D_MODEL + SSD_CONV_CH
    glu_ref[...] = (seg(o, D_MODEL) * _sigmoid(seg(o + D_MODEL, D_MODEL))).astype(BF16)
    gs_ref[...] = _sigmoid(seg(o + 2 * D_MODEL, D_MODEL)).astype(BF16)
    gc_ref[...] = _sigmoid(seg(o + 3 * D_MODEL, D_MODEL)).astype(BF16)
    dt_ref[...] = jax.nn.softplus(_dot(u_scr[...], wdt_ref[...]) + dtb_ref[...])


def _inproj(x2, g, w_main, w_dt, dt_bias):
    t = x2.shape[0]
    tm = ROW_TILE
    n_main = w_main.shape[1]
    row = lambda n: pl.BlockSpec((tm, n), lambda i: (i, 0))
    outs = [(D_MODEL, BF16), (SSD_CONV_CH, BF16), (LANES, F32), (D_MODEL, BF16), (D_MODEL, BF16), (D_MODEL, BF16)]
    return pl.pallas_call(
        _inproj_kernel,
        grid=(t // tm,),
        in_specs=[row(D_MODEL), _const_spec((1, D_MODEL)), _const_spec((D_MODEL, n_main)),
                  _const_spec((D_MODEL, LANES)), _const_spec((1, LANES))],
        out_specs=[row(n) for n, _ in outs],
        out_shape=[jax.ShapeDtypeStruct((t, n), d) for n, d in outs],
        scratch_shapes=[pltpu.VMEM((tm, D_MODEL), BF16)],
        compiler_params=_params(("parallel",)),
        name="inproj",
    )(x2, g, w_main, w_dt, dt_bias)


def _ssd_kernel(z_ref, xbc_ref, dt_ref, gs_ref, cw_ref, cb_ref, alog_ref, dfull_ref, expand_ref,
                ng_ref, wout_ref, ys_ref, ext_scr, act_scr, state_scr, y_scr):
    tc = z_ref.shape[0]
    q = SSD_CHUNK

    @pl.when(pl.program_id(1) == 0)
    def _():
        ext_scr[0:SUBLANES, :] = jnp.zeros((SUBLANES, SSD_CONV_CH), F32)
        state_scr[...] = jnp.zeros_like(state_scr)

    ext_scr[SUBLANES:SUBLANES + tc, :] = xbc_ref[...].astype(F32)
    blk = 512
    for j in range(SSD_CONV_CH // blk):
        cs = slice(j * blk, (j + 1) * blk)
        acc = jnp.broadcast_to(cb_ref[:, cs], (tc, blk))
        for k in range(SSD_CONV):
            acc = acc + cw_ref[k:k + 1, cs] * ext_scr[pl.ds(SUBLANES - (SSD_CONV - 1) + k, tc), cs]
        act_scr[:, cs] = _silu(acc)
    ext_scr[0:SUBLANES, :] = ext_scr[tc:tc + SUBLANES, :]

    rowi = lax.broadcasted_iota(jnp.int32, (q, q), 0)
    coli = lax.broadcasted_iota(jnp.int32, (q, q), 1)
    causal = coli <= rowi
    tril = causal.astype(BF16)
    left = coli < SSD_HEADDIM
    a_neg = -jnp.exp(alog_ref[...])

    def chunk(c, carry):
        r0 = pl.multiple_of(c * q, q)
        rows = pl.ds(r0, q)
        dt = dt_ref[rows, :]
        parts = _split_bf16(dt * a_neg, 3)
        a_cs = _dot(tril, parts[0]) + _dot(tril, parts[1]) + _dot(tril, parts[2])
        a_cs_t = a_cs.T
        dt_t = dt.T
        w_t = jnp.exp(a_cs_t[:, q - 1:q] - a_cs_t) * dt_t
        cd8 = jnp.broadcast_to(jnp.exp(a_cs[q - 1:q, :]), (SUBLANES, LANES))
        cdp = _split_bf16(cd8, 3)
        cd_full = (_dot(cdp[0], expand_ref[...]) + _dot(cdp[1], expand_ref[...])
                   + _dot(cdp[2], expand_ref[...]))[0:1, :]

        for g in range(SSD_GROUPS):
            bg = act_scr[rows, D_MODEL + g * SSD_STATE:D_MODEL + (g + 1) * SSD_STATE]
            cg = act_scr[rows, D_MODEL + SSD_BC + g * SSD_STATE:D_MODEL + SSD_BC + (g + 1) * SSD_STATE]
            cb = lax.dot_general(cg.astype(BF16), bg.astype(BF16), (((1,), (1,)), ((), ())),
                                 preferred_element_type=F32)
            bg_t = bg.T
            slabs_per_group = N_SLABS // SSD_GROUPS
            for jj in range(slabs_per_group):
                j = g * slabs_per_group + jj
                ls = slice(j * LANES, (j + 1) * LANES)
                m_parts, c_parts, b_parts = [], [], []
                for hh in range(HEADS_PER_SLAB):
                    h = j * HEADS_PER_SLAB + hh
                    col = a_cs[:, h:h + 1]
                    seg = col - a_cs_t[h:h + 1, :]
                    dec = jnp.exp(jnp.where(causal, seg, NEG_BIG))
                    m_parts.append((cb * dec * dt_t[h:h + 1, :]).astype(BF16))
                    c_parts.append((cg * jnp.exp(col)).astype(BF16))
                    b_parts.append((bg_t * w_t[h:h + 1, :]).astype(BF16))
                xs = act_scr[rows, ls]
                xs_b = xs.astype(BF16)
                st = state_scr[:, ls]
                st_b = st.astype(BF16)
                zero = jnp.zeros_like(xs_b)
                rhs_x = jnp.concatenate([jnp.where(left, xs_b, zero), jnp.where(left, zero, xs_b)], axis=0)
                rhs_s = jnp.concatenate([jnp.where(left, st_b, zero), jnp.where(left, zero, st_b)], axis=0)
                y = (_dot(jnp.concatenate(m_parts, axis=1), rhs_x)
                     + _dot(jnp.concatenate(c_parts, axis=1), rhs_s)
                     + dfull_ref[:, ls] * xs)
                y_scr[rows, ls] = y
                state_scr[:, ls] = st * cd_full[:, ls] + _dot(jnp.concatenate(b_parts, axis=1), rhs_x)
        return carry

    lax.fori_loop(0, tc // q, chunk, 0)

    v = y_scr[...] * _silu(z_ref[...].astype(F32))
    v = _rms(v, ng_ref[...])
    ys_ref[...] = (_dot(v.astype(BF16), wout_ref[...]) * gs_ref[...].astype(F32)).astype(BF16)


def _ssd(z, xbc, dt, gs, conv_w, conv_b, a_log, d_full, expand, norm_g, w_out, batch, seq):
    tc = ROW_TILE
    nst = seq // tc
    row = lambda n: pl.BlockSpec((tc, n), lambda b, s: (b * nst + s, 0))
    return pl.pallas_call(
        _ssd_kernel,
        grid=(batch, nst),
        in_specs=[row(D_MODEL), row(SSD_CONV_CH), row(LANES), row(D_MODEL),
                  _const_spec((SUBLANES, SSD_CONV_CH)), _const_spec((1, SSD_CONV_CH)),
                  _const_spec((1, LANES)), _const_spec((1, D_MODEL)), _const_spec((LANES, D_MODEL)),
                  _const_spec((1, D_MODEL)), _const_spec((D_MODEL, D_MODEL))],
        out_specs=row(D_MODEL),
        out_shape=jax.ShapeDtypeStruct((batch * seq, D_MODEL), BF16),
        scratch_shapes=[pltpu.VMEM((SUBLANES + tc + SUBLANES, SSD_CONV_CH), F32),
                        pltpu.VMEM((tc, SSD_CONV_CH), F32),
                        pltpu.VMEM((SSD_STATE, D_MODEL), F32),
                        pltpu.VMEM((tc, D_MODEL), F32)],
        compiler_params=_params(("arbitrary", "arbitrary")),
        name="ssd",
    )(z, xbc, dt, gs, conv_w, conv_b, a_log, d_full, expand, norm_g, w_out)


def _merge_kernel(glu_ref, ys_ref, gc_ref, x_ref, dw_ref, db_ref, lng_ref, lnb_ref, wc_ref, wo_ref,
                  nf_ref, wrh_ref, wrl_ref, rb_ref, h_ref, u_ref, lg_ref, ext_scr, c_scr):
    tm = x_ref.shape[0]

    @pl.when(pl.program_id(1) == 0)
    def _():
        ext_scr[:, 0:CONF_HALO, :] = jnp.zeros((N_SLABS, CONF_HALO, LANES), F32)

    for j in range(N_SLABS):
        ext_scr[j, CONF_HALO:CONF_HALO + tm, :] = glu_ref[:, j * LANES:(j + 1) * LANES].astype(F32)

    rc = 128
    first = CONF_HALO - (CONF_KERNEL - 1)

    def slab(j, carry):
        for r in range(tm // rc):
            acc = jnp.zeros((rc, LANES), F32)
            for k in range(CONF_KERNEL):
                acc = acc + dw_ref[j, k:k + 1, :] * ext_scr[j, pl.ds(r * rc + first + k, rc), :]
            c_scr[j, r * rc:(r + 1) * rc, :] = acc
        return carry

    lax.fori_loop(0, N_SLABS, slab, 0)
    for j in range(N_SLABS):
        ext_scr[j, 0:CONF_HALO, :] = ext_scr[j, tm:tm + CONF_HALO, :]

    c = jnp.concatenate([c_scr[j] for j in range(N_SLABS)], axis=1) + db_ref[...]
    mu = jnp.mean(c, axis=-1, keepdims=True)
    xc = c - mu
    yln = xc * lax.rsqrt(jnp.mean(xc * xc, axis=-1, keepdims=True) + EPS) * lng_ref[...] + lnb_ref[...]
    y_conv = _dot(_silu(yln).astype(BF16), wc_ref[...])
    mix = gc_ref[...].astype(F32) * y_conv + ys_ref[...].astype(F32)
    h = x_ref[...] + _dot(mix.astype(BF16), wo_ref[...])
    h_ref[...] = h
    u = _rms(h, nf_ref[...])
    u_ref[...] = u
    u_hi, u_lo = _split_bf16(u, 2)
    lg_ref[...] = (_dot(u_hi, wrh_ref[...]) + _dot(u_hi, wrl_ref[...]) + _dot(u_lo, wrh_ref[...])
                   + rb_ref[...])


def _merge(glu, ys, gc, x2, dw3, dw_b, ln_g, ln_b, w_conf, w_o, nf_g, wr_hi, wr_lo, rb, batch, seq):
    tm = ROW_TILE
    nst = seq // tm
    t = batch * seq
    row = lambda n: pl.BlockSpec((tm, n), lambda b, s: (b * nst + s, 0))
    outs = [(D_MODEL, F32), (D_MODEL, F32), (LANES, F32)]
    return pl.pallas_call(
        _merge_kernel,
        grid=(batch, nst),
        in_specs=[row(D_MODEL), row(D_MODEL), row(D_MODEL), row(D_MODEL),
                  _const_spec((N_SLABS, CONF_HALO, LANES)), _const_spec((1, D_MODEL)),
                  _const_spec((1, D_MODEL)), _const_spec((1, D_MODEL)),
                  _const_spec((D_MODEL, D_MODEL)), _const_spec((D_MODEL, D_MODEL)),
                  _const_spec((1, D_MODEL)), _const_spec((D_MODEL, LANES)), _const_spec((D_MODEL, LANES)),
                  _const_spec((1, LANES))],
        out_specs=[row(n) for n, _ in outs],
        out_shape=[jax.ShapeDtypeStruct((t, n), d) for n, d in outs],
        scratch_shapes=[pltpu.VMEM((N_SLABS, CONF_HALO + tm, LANES), F32),
                        pltpu.VMEM((N_SLABS, tm, LANES), F32)],
        compiler_params=_params(("arbitrary", "arbitrary")),
        name="merge",
    )(glu, ys, gc, x2, dw3, dw_b, ln_g, ln_b, w_conf, w_o, nf_g, wr_hi, wr_lo, rb)


META_E0, META_E1, META_R0, META_R1, META_W0, META_W1 = 0, 1, 2, 3, 4, 5


def _router_kernel(lg_ref, wrow_ref, idx_ref, cnt_ref, base_scr):
    tm = lg_ref.shape[0]

    @pl.when(pl.program_id(0) == 0)
    def _():
        base_scr[...] = jnp.zeros_like(base_scr)

    lg = lg_ref[...]
    lane = lax.broadcasted_iota(jnp.int32, (tm, LANES), 1)
    far = jnp.int32(4 * LANES)

    def first_argmax(v):
        m = jnp.max(v, axis=-1, keepdims=True)
        return m, jnp.min(jnp.where(v == m, lane, far), axis=-1, keepdims=True)

    is_group = (lane >= N_EXPERTS) & (lane < N_EXPERTS + N_GROUPS)
    gl = jnp.where(is_group, lg, NEG_BIG)
    gmax, glane = first_argmax(gl)
    gidx = glane - N_EXPERTS
    g_prob = 1.0 / jnp.sum(jnp.where(is_group, jnp.exp(gl - gmax), 0.0), axis=-1, keepdims=True)

    in_group = (lane < N_EXPERTS) & ((lane // EXPERTS_PER_GROUP) == gidx)
    el = jnp.where(in_group, lg, NEG_BIG)
    m0, e0 = first_argmax(el)
    m1, e1 = first_argmax(jnp.where(lane == e0, NEG_BIG, el))
    ex = jnp.exp(m1 - m0)
    w0 = g_prob / (1.0 + ex)
    w1 = w0 * ex

    a0 = (lane == e0).astype(F32)
    a1 = (lane == e1).astype(F32)
    ti = lax.broadcasted_iota(jnp.int32, (tm, tm), 0)
    tj = lax.broadcasted_iota(jnp.int32, (tm, tm), 1)
    before = (tj < ti).astype(BF16)
    base = base_scr[...]
    c0 = jnp.sum(a0, axis=0, keepdims=True)
    c1 = jnp.sum(a1, axis=0, keepdims=True)
    r0 = jnp.sum(a0 * (_dot(before, a0.astype(BF16)) + base), axis=-1, keepdims=True)
    r1 = jnp.sum(a1 * (_dot(before, a1.astype(BF16)) + base + c0), axis=-1, keepdims=True)
    base_scr[...] = base + c0 + c1
    cnt_ref[...] = base + c0 + c1

    meta = jnp.zeros((tm, LANES), F32)
    for pos, val in ((META_E0, e0.astype(F32)), (META_E1, e1.astype(F32)), (META_R0, r0), (META_R1, r1),
                     (META_W0, w0), (META_W1, w1)):
        meta = jnp.where(lane == pos, val, meta)
    wrow_ref[...] = meta
    idx_ref[...] = meta.T[0:SUBLANES, :].astype(jnp.int32)


def _router(logits):
    t = logits.shape[0]
    tm = ROW_TILE
    return pl.pallas_call(
        _router_kernel,
        grid=(t // tm,),
        in_specs=[pl.BlockSpec((tm, LANES), lambda i: (i, 0))],
        out_specs=[pl.BlockSpec((tm, LANES), lambda i: (i, 0)),
                   pl.BlockSpec((SUBLANES, tm), lambda i: (0, i)),
                   pl.BlockSpec((1, LANES), lambda i: (0, 0))],
        out_shape=[jax.ShapeDtypeStruct((t, LANES), F32),
                   jax.ShapeDtypeStruct((SUBLANES, t), jnp.int32),
                   jax.ShapeDtypeStruct((1, LANES), F32)],
        scratch_shapes=[pltpu.VMEM((1, LANES), F32)],
        compiler_params=_params(("arbitrary",)),
        name="router",
    )(logits)


def _dispatch_kernel(dest_ref, u_ref, buf_in_ref, buf_ref, sem):
    del buf_in_ref
    tm = u_ref.shape[0]

    def row_copy(j, k):
        return pltpu.make_async_copy(u_ref.at[pl.ds(j, 1)], buf_ref.at[pl.ds(dest_ref[0, k, j], 1)], sem)

    def issue(j, carry):
        row_copy(j, 0).start()
        row_copy(j, 1).start()
        return carry

    lax.fori_loop(0, tm, issue, 0)
    for _ in range(2):
        pltpu.make_async_copy(u_ref, buf_ref.at[pl.ds(0, tm)], sem).wait()


def _dispatch(dest3, u, buf0):
    t = u.shape[0]
    tm = ROW_TILE
    return pl.pallas_call(
        _dispatch_kernel,
        grid=(t // tm,),
        in_specs=[pl.BlockSpec((1, 2, tm), lambda i: (i, 0, 0), memory_space=pltpu.SMEM),
                  pl.BlockSpec((tm, D_MODEL), lambda i: (i, 0)),
                  pl.BlockSpec(memory_space=pl.ANY)],
        out_specs=pl.BlockSpec(memory_space=pl.ANY),
        out_shape=jax.ShapeDtypeStruct(buf0.shape, buf0.dtype),
        scratch_shapes=[pltpu.SemaphoreType.DMA(())],
        input_output_aliases={2: 0},
        compiler_params=_params(("arbitrary",)),
        name="dispatch",
    )(dest3, u, buf0)


def _ffn_kernel(be_ref, nv_ref, x_ref, wg_ref, wu_ref, wd_ref, y_ref):
    del be_ref

    @pl.when(pl.program_id(0) < nv_ref[0])
    def _():
        x = x_ref[...].astype(BF16)
        a = _dot(x, wg_ref[0])
        b = _dot(x, wu_ref[0])
        y_ref[...] = _dot((_silu(a) * b).astype(BF16), wd_ref[0])

    @pl.when(pl.program_id(0) >= nv_ref[0])
    def _():
        y_ref[...] = jnp.zeros_like(y_ref)


def _ffn(blk_e, n_valid, buf, wg, wu, wd):
    rows = buf.shape[0]
    bk = MOE_BLOCK_ROWS
    xmap = lambda i, be, nv: (jnp.minimum(i, nv[0] - 1), 0)
    ymap = lambda i, be, nv: (i, 0)
    wmap = lambda i, be, nv: (be[i], 0, 0)
    return pl.pallas_call(
        _ffn_kernel,
        grid_spec=pltpu.PrefetchScalarGridSpec(
            num_scalar_prefetch=2,
            grid=(rows // bk,),
            in_specs=[pl.BlockSpec((bk, D_MODEL), xmap),
                      pl.BlockSpec((1, D_MODEL, D_EXPERT), wmap),
                      pl.BlockSpec((1, D_MODEL, D_EXPERT), wmap),
                      pl.BlockSpec((1, D_EXPERT, D_MODEL), wmap)],
            out_specs=pl.BlockSpec((bk, D_MODEL), ymap)),
        out_shape=jax.ShapeDtypeStruct((rows, D_MODEL), F32),
        compiler_params=_params(("arbitrary",)),
        name="expert_ffn",
    )(blk_e, n_valid, buf, wg, wu, wd)


def _combine_kernel(dest_ref, wrow_ref, h_ref, p_ref, yb_ref, gp_ref, wpg_ref, wpp_ref, gf_ref,
                    o_ref, g0_scr, g1_scr, sem):
    tm = h_ref.shape[0]

    def row_copy(j, k, dst):
        return pltpu.make_async_copy(yb_ref.at[pl.ds(dest_ref[0, k, j], 1)], dst.at[pl.ds(j, 1)], sem)

    def issue(j, carry):
        row_copy(j, 0, g0_scr).start()
        row_copy(j, 1, g1_scr).start()
        return carry

    lax.fori_loop(0, tm, issue, 0)
    for dst in (g0_scr, g1_scr):
        pltpu.make_async_copy(yb_ref.at[pl.ds(0, tm)], dst, sem).wait()

    w = wrow_ref[...]
    h = (h_ref[...] + w[:, META_W0:META_W0 + 1] * g0_scr[...] + w[:, META_W1:META_W1 + 1] * g1_scr[...])
    gate = _sigmoid(_dot(_rms(h, gp_ref[...]).astype(BF16), wpg_ref[...]))
    h = h + _dot(p_ref[...].astype(BF16), wpp_ref[...]) * gate
    o_ref[...] = _rms(h, gf_ref[...])


def _combine(dest3, wrow, h1, p2, yb, g_ple, w_pg, w_pp, g_fin):
    t = h1.shape[0]
    tm = ROW_TILE
    row = lambda n: pl.BlockSpec((tm, n), lambda i: (i, 0))
    return pl.pallas_call(
        _combine_kernel,
        grid=(t // tm,),
        in_specs=[pl.BlockSpec((1, 2, tm), lambda i: (i, 0, 0), memory_space=pltpu.SMEM),
                  row(LANES), row(D_MODEL), row(PLE_DIM),
                  pl.BlockSpec(memory_space=pl.ANY),
                  _const_spec((1, D_MODEL)), _const_spec((D_MODEL, D_MODEL)),
                  _const_spec((PLE_DIM, D_MODEL)), _const_spec((1, D_MODEL))],
        out_specs=row(D_MODEL),
        out_shape=jax.ShapeDtypeStruct((t, D_MODEL), F32),
        scratch_shapes=[pltpu.VMEM((tm, D_MODEL), F32), pltpu.VMEM((tm, D_MODEL), F32),
                        pltpu.SemaphoreType.DMA(())],
        compiler_params=_params(("arbitrary",)),
        name="combine",
    )(dest3, wrow, h1, p2, yb, g_ple, w_pg, w_pp, g_fin)


def _row(v, n=None):
    v = v.astype(F32).reshape(1, -1)
    if n is not None and v.shape[1] < n:
        v = jnp.pad(v, ((0, 0), (0, n - v.shape[1])))
    return v


def _layer(h2, p2, batch, seq, norm_mix_g, w_in, ssd_conv_w, ssd_conv_b, ssd_dt_bias, ssd_a_log, ssd_d,
           ssd_norm_g, w_ssd_out, conf_dw_w, conf_dw_b, conf_ln_g, conf_ln_b, w_conf_out, w_o,
           norm_ffn_g, router_group_w, router_group_b, router_expert_w, router_expert_b,
           expert_w_gate, expert_w_up, expert_w_down, norm_ple_g, w_ple_gate, w_ple_proj, final_g):
    t = batch * seq
    w_main = jnp.concatenate([w_in[:, :OFF_DT], w_in[:, OFF_GLU:]], axis=1).astype(BF16)
    w_dt = jnp.pad(w_in[:, OFF_DT:OFF_GLU], ((0, 0), (0, LANES - SSD_HEADS))).astype(BF16)
    conv_w = jnp.pad(ssd_conv_w.astype(F32), ((0, SUBLANES - SSD_CONV), (0, 0)))
    d_full = jnp.repeat(ssd_d.astype(F32), SSD_HEADDIM).reshape(1, D_MODEL)
    head_of_lane = jnp.arange(D_MODEL) // SSD_HEADDIM
    expand = (jnp.arange(LANES)[:, None] == head_of_lane[None, :]).astype(BF16)
    dw3 = jnp.pad(conf_dw_w.astype(F32), ((0, CONF_HALO - CONF_KERNEL), (0, 0)))
    dw3 = dw3.reshape(CONF_HALO, N_SLABS, LANES).transpose(1, 0, 2)
    wr = jnp.concatenate([router_expert_w, router_group_w], axis=1).astype(F32)
    wr = jnp.pad(wr, ((0, 0), (0, LANES - wr.shape[1])))
    wr_hi = wr.astype(BF16)
    wr_lo = (wr - wr_hi.astype(F32)).astype(BF16)
    rb = _row(jnp.concatenate([router_expert_b, router_group_b]), LANES)

    z, xbc, dt, glu, gs, gc = _inproj(h2, _row(norm_mix_g), w_main, w_dt, _row(ssd_dt_bias, LANES))
    ys = _ssd(z, xbc, dt, gs, conv_w, _row(ssd_conv_b), _row(ssd_a_log, LANES), d_full, expand,
              _row(ssd_norm_g), w_ssd_out.astype(BF16), batch, seq)
    h1, u, logits = _merge(glu, ys, gc, h2, dw3, _row(conf_dw_b), _row(conf_ln_g), _row(conf_ln_b),
                           w_conf_out.astype(BF16), w_o.astype(BF16), _row(norm_ffn_g), wr_hi, wr_lo, rb,
                           batch, seq)
    wrow, idx, counts = _router(logits)

    bk = MOE_BLOCK_ROWS
    n_blocks = (2 * t) // bk + N_EXPERTS
    cnt = counts[0, :N_EXPERTS].astype(jnp.int32)
    padded = (cnt + bk - 1) // bk * bk
    pad_end = jnp.cumsum(padded)
    pad_start = pad_end - padded
    n_valid = (pad_end[-1] // bk).astype(jnp.int32)
    blk = jnp.arange(n_blocks, dtype=jnp.int32)
    blk_e = jnp.sum((blk[:, None] * bk >= pad_end[None, :]).astype(jnp.int32), axis=1)
    blk_e = jnp.minimum(blk_e, N_EXPERTS - 1)
    last_e = jnp.sum((((n_valid - 1) * bk) >= pad_end).astype(jnp.int32))
    blk_e = jnp.where(blk < n_valid, blk_e, jnp.minimum(last_e, N_EXPERTS - 1)).astype(jnp.int32)
    onehot = lambda e: (e[:, None] == jnp.arange(N_EXPERTS, dtype=jnp.int32)[None, :]).astype(jnp.int32)
    dest0 = idx[META_R0] + jnp.sum(onehot(idx[META_E0]) * pad_start[None, :], axis=1)
    dest1 = idx[META_R1] + jnp.sum(onehot(idx[META_E1]) * pad_start[None, :], axis=1)
    tm = ROW_TILE
    dest3 = jnp.stack([dest0.reshape(t // tm, tm), dest1.reshape(t // tm, tm)], axis=1).astype(jnp.int32)

    buf = _dispatch(dest3, u, jnp.zeros((n_blocks * bk, D_MODEL), F32))
    yb = _ffn(blk_e, n_valid.reshape(1), buf, expert_w_gate.astype(BF16), expert_w_up.astype(BF16),
              expert_w_down.astype(BF16))
    return _combine(dest3, wrow, h1, p2, yb, _row(norm_ple_g), w_ple_gate.astype(BF16),
                    w_ple_proj.astype(BF16), final_g)


def kernel(x, p, norm_mix_g, w_in, ssd_conv_w, ssd_conv_b, ssd_dt_bias, ssd_a_log, ssd_d, ssd_norm_g,
           w_ssd_out, conf_dw_w, conf_dw_b, conf_ln_g, conf_ln_b, w_conf_out, w_o, norm_ffn_g,
           router_group_w, router_group_b, router_expert_w, router_expert_b, expert_w_gate, expert_w_up,
           expert_w_down, norm_ple_g, w_ple_gate, w_ple_proj, final_norm_g):
    batch, seq, d = x.shape
    depth = p.shape[0]
    assert d == D_MODEL and depth == 1, "single-layer block with D_MODEL features"
    assert seq % ROW_TILE == 0 and ROW_TILE % SSD_CHUNK == 0
    h2 = x.reshape(batch * seq, d)
    out = _layer(h2, p[0].reshape(batch * seq, PLE_DIM), batch, seq, norm_mix_g[0], w_in[0], ssd_conv_w[0],
                 ssd_conv_b[0], ssd_dt_bias[0], ssd_a_log[0], ssd_d[0], ssd_norm_g[0], w_ssd_out[0],
                 conf_dw_w[0], conf_dw_b[0], conf_ln_g[0], conf_ln_b[0], w_conf_out[0], w_o[0],
                 norm_ffn_g[0], router_group_w[0], router_group_b[0], router_expert_w[0],
                 router_expert_b[0], expert_w_gate[0], expert_w_up[0], expert_w_down[0], norm_ple_g[0],
                 w_ple_gate[0], w_ple_proj[0], _row(final_norm_g))
    return out.reshape(batch, seq, d)
```

```python
import jax
import jax.numpy as jnp
from jax import lax
from jax.experimental import pallas as pl
from jax.experimental.pallas import tpu as pltpu

F32 = jnp.float32
BF16 = jnp.bfloat16
I32 = jnp.int32
EPS = 1e-6

D_MODEL = 1024
PLE_DIM = 256
SSD_HEADDIM = 64
SSD_HEADS = 16
SSD_GROUPS = 2
SSD_STATE = 128
SSD_CONV = 4
SSD_CHUNK = 128
SSD_BC = SSD_GROUPS * SSD_STATE
SSD_CONV_CH = D_MODEL + 2 * SSD_BC
CONF_KERNEL = 31
N_GROUPS = 4
EXPERTS_PER_GROUP = 8
N_EXPERTS = 32
D_EXPERT = 512

OFF_XBC = D_MODEL
OFF_DT = OFF_XBC + SSD_CONV_CH
OFF_GLU = OFF_DT + SSD_HEADS
OFF_GATE = OFF_GLU + 2 * D_MODEL

LANES = 128
SUBLANES = 8
VMEM_LIMIT_BYTES = 56 * 1024 * 1024

ROW_TILE = 512
MOE_BLOCK_ROWS = 512
NEG_BIG = -1e30
HEADS_PER_SLAB = LANES // SSD_HEADDIM
N_SLABS = D_MODEL // LANES
XBC_SLABS = SSD_CONV_CH // LANES
CONF_HALO = 32
HALF = D_MODEL // 2
LOCAL_ROWS = 2 * ROW_TILE + N_EXPERTS * SUBLANES
HIGH_MASK = -65536
COPY_WINDOW = 128


def _sigmoid(v):
    return jax.nn.sigmoid(v)


def _silu(v):
    return v * jax.nn.sigmoid(v)


def _rms(v, g):
    return v * lax.rsqrt(jnp.mean(v * v, axis=-1, keepdims=True) + EPS) * g


def _dot(a, b):
    return jnp.dot(a, b, preferred_element_type=F32)


def _split_bf16(v, parts):
    out = []
    r = v
    for _ in range(parts):
        h = r.astype(BF16)
        out.append(h)
        r = r - h.astype(F32)
    return out


def _pack_pairs(v):
    r = v.astype(BF16).astype(F32)
    lo_bits = lax.bitcast_convert_type(r[:, :HALF], I32)
    lo = lax.shift_right_logical(lo_bits, jnp.full_like(lo_bits, 16))
    hi = lax.bitcast_convert_type(r[:, HALF:], I32) & HIGH_MASK
    return hi | lo


def _unpack_pairs(w):
    lo = lax.bitcast_convert_type(lax.shift_left(w, jnp.full_like(w, 16)), F32)
    hi = lax.bitcast_convert_type(w & HIGH_MASK, F32)
    return lo.astype(BF16), hi.astype(BF16)


def _const_spec(shape):
    nd = len(shape)
    return pl.BlockSpec(shape, lambda *_: (0,) * nd, pipeline_mode=pl.Buffered(1))


def _params(sem):
    return pltpu.CompilerParams(dimension_semantics=sem, vmem_limit_bytes=VMEM_LIMIT_BYTES)


def _inproj_kernel(x_ref, g_ref, w_ref, wdt_ref, dtb_ref,
                   z_ref, xbc_ref, dt_ref, glu_ref, gs_ref, gc_ref, u_scr):
    u_scr[...] = _rms(x_ref[...], g_ref[...]).astype(BF16)

    def seg(lo, n):
        return _dot(u_scr[...], w_ref[:, lo:lo + n])

    z_ref[...] = seg(0, D_MODEL).astype(BF16)
    blk = 512
    for j in range(SSD_CONV_CH // blk):
        xbc_ref[:, j * blk:(j + 1) * blk] = seg(D_MODEL + j * blk, blk).astype(BF16)
    o = D_MODEL + SSD_CONV_CH
    glu_ref[...] = (seg(o, D_MODEL) * _sigmoid(seg(o + D_MODEL, D_MODEL))).astype(BF16)
    gs_ref[...] = _sigmoid(seg(o + 2 * D_MODEL, D_MODEL)).astype(BF16)
    gc_ref[...] = _sigmoid(seg(o + 3 * D_MODEL, D_MODEL)).astype(BF16)
    dt_ref[...] = jax.nn.softplus(_dot(u_scr[...], wdt_ref[...]) + dtb_ref[...])


def _inproj(x2, g, w_main, w_dt, dt_bias):
    t = x2.shape[0]
    tm = ROW_TILE
    n_main = w_main.shape[1]
    row = lambda n: pl.BlockSpec((tm, n), lambda i: (i, 0))
    outs = [(D_MODEL, BF16), (SSD_CONV_CH, BF16), (LANES, F32), (D_MODEL, BF16), (D_MODEL, BF16), (D_MODEL, BF16)]
    return pl.pallas_call(
        _inproj_kernel,
        grid=(t // tm,),
        in_specs=[row(D_MODEL), _const_spec((1, D_MODEL)), _const_spec((D_MODEL, n_main)),
                  _const_spec((D_MODEL, LANES)), _const_spec((1, LANES))],
        out_specs=[row(n) for n, _ in outs],
        out_shape=[jax.ShapeDtypeStruct((t, n), d) for n, d in outs],
        scratch_shapes=[pltpu.VMEM((tm, D_MODEL), BF16)],
        compiler_params=_params(("parallel",)),
        name="inproj",
    )(x2, g, w_main, w_dt, dt_bias)


CONV_BIAS_ROW = SSD_CONV


def _ssd_kernel(z_ref, xbc_ref, dt_ref, gs_ref, cw_ref, alog_ref, dfull_ref, expand_ref,
                ng_ref, wout_ref, ys_ref, ext_scr, act_scr, state_scr, y_scr):
    tc = z_ref.shape[0]
    q = SSD_CHUNK

    @pl.when(pl.program_id(1) == 0)
    def _():
        ext_scr[:, 0:SUBLANES, :] = jnp.zeros((XBC_SLABS, SUBLANES, LANES), F32)
        state_scr[...] = jnp.zeros_like(state_scr)

    for j in range(XBC_SLABS):
        ext_scr[j, SUBLANES:SUBLANES + tc, :] = xbc_ref[:, j * LANES:(j + 1) * LANES].astype(F32)
    first = SUBLANES - (SSD_CONV - 1)

    def conv_slab(j, carry):
        for r in range(tc // q):
            acc = jnp.broadcast_to(cw_ref[j, CONV_BIAS_ROW:CONV_BIAS_ROW + 1, :], (q, LANES))
            for k in range(SSD_CONV):
                acc = acc + cw_ref[j, k:k + 1, :] * ext_scr[j, pl.ds(r * q + first + k, q), :]
            act_scr[j, r * q:(r + 1) * q, :] = _silu(acc)
        return carry

    lax.fori_loop(0, XBC_SLABS, conv_slab, 0)
    for j in range(XBC_SLABS):
        ext_scr[j, 0:SUBLANES, :] = ext_scr[j, tc:tc + SUBLANES, :]

    rowi = lax.broadcasted_iota(I32, (q, q), 0)
    coli = lax.broadcasted_iota(I32, (q, q), 1)
    causal = coli <= rowi
    tril = causal.astype(BF16)
    left = coli < SSD_HEADDIM
    a_neg = -jnp.exp(alog_ref[...])

    def chunk(c, carry):
        r0 = pl.multiple_of(c * q, q)
        rows = pl.ds(r0, q)
        dt = dt_ref[rows, :]
        parts = _split_bf16(dt * a_neg, 3)
        a_cs = _dot(tril, parts[0]) + _dot(tril, parts[1]) + _dot(tril, parts[2])
        a_cs_t = a_cs.T
        dt_t = dt.T
        w_t = jnp.exp(a_cs_t[:, q - 1:q] - a_cs_t) * dt_t
        cd8 = jnp.broadcast_to(jnp.exp(a_cs[q - 1:q, :]), (SUBLANES, LANES))
        cdp = _split_bf16(cd8, 3)
        cd_full = (_dot(cdp[0], expand_ref[...]) + _dot(cdp[1], expand_ref[...])
                   + _dot(cdp[2], expand_ref[...]))[0:1, :]

        slabs_per_group = N_SLABS // SSD_GROUPS
        for g in range(SSD_GROUPS):
            bg = act_scr[N_SLABS + g, rows, :]
            cg = act_scr[N_SLABS + SSD_GROUPS + g, rows, :]
            cb = lax.dot_general(cg.astype(BF16), bg.astype(BF16), (((1,), (1,)), ((), ())),
                                 preferred_element_type=F32)
            bg_t = bg.T
            for jj in range(slabs_per_group):
                j = g * slabs_per_group + jj
                ls = slice(j * LANES, (j + 1) * LANES)
                m_parts, c_parts, b_parts = [], [], []
                for hh in range(HEADS_PER_SLAB):
                    h = j * HEADS_PER_SLAB + hh
                    col = a_cs[:, h:h + 1]
                    seg = col - a_cs_t[h:h + 1, :]
                    dec = jnp.exp(jnp.where(causal, seg, NEG_BIG))
                    m_parts.append((cb * dec * dt_t[h:h + 1, :]).astype(BF16))
                    c_parts.append((cg * jnp.exp(col)).astype(BF16))
                    b_parts.append((bg_t * w_t[h:h + 1, :]).astype(BF16))
                xs = act_scr[j, rows, :]
                xs_b = xs.astype(BF16)
                st = state_scr[:, ls]
                st_b = st.astype(BF16)
                zero = jnp.zeros_like(xs_b)
                rhs_x = jnp.concatenate([jnp.where(left, xs_b, zero), jnp.where(left, zero, xs_b)], axis=0)
                rhs_s = jnp.concatenate([jnp.where(left, st_b, zero), jnp.where(left, zero, st_b)], axis=0)
                y = (_dot(jnp.concatenate(m_parts, axis=1), rhs_x)
                     + _dot(jnp.concatenate(c_parts, axis=1), rhs_s)
                     + dfull_ref[:, ls] * xs)
                y_scr[rows, ls] = y
                state_scr[:, ls] = st * cd_full[:, ls] + _dot(jnp.concatenate(b_parts, axis=1), rhs_x)
        return carry

    lax.fori_loop(0, tc // q, chunk, 0)

    v = y_scr[...] * _silu(z_ref[...].astype(F32))
    v = _rms(v, ng_ref[...])
    ys_ref[...] = (_dot(v.astype(BF16), wout_ref[...]) * gs_ref[...].astype(F32)).astype(BF16)


def _ssd(z, xbc, dt, gs, conv_tab, a_log, d_full, expand, norm_g, w_out, batch, seq):
    tc = ROW_TILE
    nst = seq // tc
    row = lambda n: pl.BlockSpec((tc, n), lambda b, s: (b * nst + s, 0))
    return pl.pallas_call(
        _ssd_kernel,
        grid=(batch, nst),
        in_specs=[row(D_MODEL), row(SSD_CONV_CH), row(LANES), row(D_MODEL),
                  _const_spec((XBC_SLABS, SUBLANES, LANES)),
                  _const_spec((1, LANES)), _const_spec((1, D_MODEL)), _const_spec((LANES, D_MODEL)),
                  _const_spec((1, D_MODEL)), _const_spec((D_MODEL, D_MODEL))],
        out_specs=row(D_MODEL),
        out_shape=jax.ShapeDtypeStruct((batch * seq, D_MODEL), BF16),
        scratch_shapes=[pltpu.VMEM((XBC_SLABS, SUBLANES + tc, LANES), F32),
                        pltpu.VMEM((XBC_SLABS, tc, LANES), F32),
                        pltpu.VMEM((SSD_STATE, D_MODEL), F32),
                        pltpu.VMEM((tc, D_MODEL), F32)],
        compiler_params=_params(("arbitrary", "arbitrary")),
        name="ssd",
    )(z, xbc, dt, gs, conv_tab, a_log, d_full, expand, norm_g, w_out)


META_P0, META_P1, META_W0, META_W1 = 0, 1, 2, 3
TAB_ROWS, TAB_START = 0, 1


def _route_and_sort(u, lg, su_ref, meta_ref, tab_ref):
    tm = u.shape[0]
    lane = lax.broadcasted_iota(I32, (tm, LANES), 1)
    far = jnp.int32(4 * LANES)

    def first_argmax(v):
        m = jnp.max(v, axis=-1, keepdims=True)
        return m, jnp.min(jnp.where(v == m, lane, far), axis=-1, keepdims=True)

    is_group = (lane >= N_EXPERTS) & (lane < N_EXPERTS + N_GROUPS)
    gl = jnp.where(is_group, lg, NEG_BIG)
    gmax, glane = first_argmax(gl)
    gidx = glane - N_EXPERTS
    g_prob = 1.0 / jnp.sum(jnp.where(is_group, jnp.exp(gl - gmax), 0.0), axis=-1, keepdims=True)

    in_group = (lane < N_EXPERTS) & ((lane // EXPERTS_PER_GROUP) == gidx)
    el = jnp.where(in_group, lg, NEG_BIG)
    m0, e0 = first_argmax(el)
    m1, e1 = first_argmax(jnp.where(lane == e0, NEG_BIG, el))
    ex = jnp.exp(m1 - m0)
    w0 = g_prob / (1.0 + ex)
    w1 = w0 * ex

    a0 = (lane == e0).astype(F32)
    a1 = (lane == e1).astype(F32)
    c0 = jnp.sum(a0, axis=0, keepdims=True)
    c1 = jnp.sum(a1, axis=0, keepdims=True)
    rows8 = jnp.floor((c0 + c1 + (SUBLANES - 1)) * (1.0 / SUBLANES)) * SUBLANES
    ei = lax.broadcasted_iota(I32, (LANES, LANES), 0)
    ej = lax.broadcasted_iota(I32, (LANES, LANES), 1)
    earlier_expert = (ei < ej).astype(BF16)
    start = _dot(jnp.broadcast_to(rows8, (SUBLANES, LANES)).astype(BF16), earlier_expert)[0:1, :]
    ti = lax.broadcasted_iota(I32, (tm, tm), 0)
    tj = lax.broadcasted_iota(I32, (tm, tm), 1)
    before = (tj < ti).astype(BF16)
    p0 = jnp.sum(a0 * (_dot(before, a0.astype(BF16)) + start), axis=-1, keepdims=True)
    p1 = jnp.sum(a1 * (_dot(before, a1.astype(BF16)) + start + c0), axis=-1, keepdims=True)

    meta = jnp.zeros((tm, LANES), F32)
    for pos, val in ((META_P0, p0), (META_P1, p1), (META_W0, w0), (META_W1, w1)):
        meta = jnp.where(lane == pos, val, meta)
    meta_ref[...] = meta
    sub = lax.broadcasted_iota(I32, (SUBLANES, LANES), 0)
    tab_ref[0] = jnp.where(sub == TAB_ROWS, rows8, jnp.where(sub == TAB_START, start, 0.0))

    meta_t = meta.T
    ri = lax.broadcasted_iota(I32, (LOCAL_ROWS, tm), 0).astype(F32)
    take = (ri == meta_t[META_P0:META_P0 + 1, :]) | (ri == meta_t[META_P1:META_P1 + 1, :])
    gathered = _dot(jnp.where(take, 1.0, 0.0).astype(BF16), u.astype(BF16))
    su_ref[...] = _pack_pairs(gathered)


def _merge_kernel(glu_ref, ys_ref, gc_ref, x_ref, dw_ref, db_ref, lng_ref, lnb_ref, wc_ref, wo_ref,
                  nf_ref, wrh_ref, wrl_ref, rb_ref, h_ref, su_ref, meta_ref, tab_ref, ext_scr, c_scr):
    tm = x_ref.shape[0]

    @pl.when(pl.program_id(1) == 0)
    def _():
        ext_scr[:, 0:CONF_HALO, :] = jnp.zeros((N_SLABS, CONF_HALO, LANES), F32)

    for j in range(N_SLABS):
        ext_scr[j, CONF_HALO:CONF_HALO + tm, :] = glu_ref[:, j * LANES:(j + 1) * LANES].astype(F32)

    rc = 128
    first = CONF_HALO - (CONF_KERNEL - 1)

    def slab(j, carry):
        for r in range(tm // rc):
            acc = jnp.zeros((rc, LANES), F32)
            for k in range(CONF_KERNEL):
                acc = acc + dw_ref[j, k:k + 1, :] * ext_scr[j, pl.ds(r * rc + first + k, rc), :]
            c_scr[j, r * rc:(r + 1) * rc, :] = acc
        return carry

    lax.fori_loop(0, N_SLABS, slab, 0)
    for j in range(N_SLABS):
        ext_scr[j, 0:CONF_HALO, :] = ext_scr[j, tm:tm + CONF_HALO, :]

    c = jnp.concatenate([c_scr[j] for j in range(N_SLABS)], axis=1) + db_ref[...]
    mu = jnp.mean(c, axis=-1, keepdims=True)
    xc = c - mu
    yln = xc * lax.rsqrt(jnp.mean(xc * xc, axis=-1, keepdims=True) + EPS) * lng_ref[...] + lnb_ref[...]
    y_conv = _dot(_silu(yln).astype(BF16), wc_ref[...])
    mix = gc_ref[...].astype(F32) * y_conv + ys_ref[...].astype(F32)
    h = x_ref[...] + _dot(mix.astype(BF16), wo_ref[...])
    h_ref[...] = h
    u = _rms(h, nf_ref[...])
    u_hi, u_lo = _split_bf16(u, 2)
    lg = _dot(u_hi, wrh_ref[...]) + _dot(u_hi, wrl_ref[...]) + _dot(u_lo, wrh_ref[...]) + rb_ref[...]
    _route_and_sort(u, lg, su_ref, meta_ref, tab_ref)


def _merge(glu, ys, gc, x2, dw3, dw_b, ln_g, ln_b, w_conf, w_o, nf_g, wr_hi, wr_lo, rb, batch, seq):
    tm = ROW_TILE
    nst = seq // tm
    t = batch * seq
    ntile = t // tm
    tile = lambda b, s: b * nst + s
    row = lambda n: pl.BlockSpec((tm, n), lambda b, s: (tile(b, s), 0))
    return pl.pallas_call(
        _merge_kernel,
        grid=(batch, nst),
        in_specs=[row(D_MODEL), row(D_MODEL), row(D_MODEL), row(D_MODEL),
                  _const_spec((N_SLABS, CONF_HALO, LANES)), _const_spec((1, D_MODEL)),
                  _const_spec((1, D_MODEL)), _const_spec((1, D_MODEL)),
                  _const_spec((D_MODEL, D_MODEL)), _const_spec((D_MODEL, D_MODEL)),
                  _const_spec((1, D_MODEL)), _const_spec((D_MODEL, LANES)), _const_spec((D_MODEL, LANES)),
                  _const_spec((1, LANES))],
        out_specs=[row(D_MODEL),
                   pl.BlockSpec((LOCAL_ROWS, HALF), lambda b, s: (tile(b, s), 0)),
                   row(LANES),
                   pl.BlockSpec((1, SUBLANES, LANES), lambda b, s: (tile(b, s), 0, 0))],
        out_shape=[jax.ShapeDtypeStruct((t, D_MODEL), F32),
                   jax.ShapeDtypeStruct((ntile * LOCAL_ROWS, HALF), I32),
                   jax.ShapeDtypeStruct((t, LANES), F32),
                   jax.ShapeDtypeStruct((ntile, SUBLANES, LANES), F32)],
        scratch_shapes=[pltpu.VMEM((N_SLABS, CONF_HALO + tm, LANES), F32),
                        pltpu.VMEM((N_SLABS, tm, LANES), F32)],
        compiler_params=_params(("arbitrary", "arbitrary")),
        name="merge",
    )(glu, ys, gc, x2, dw3, dw_b, ln_g, ln_b, w_conf, w_o, nf_g, wr_hi, wr_lo, rb)


BIG_PIECE = MOE_BLOCK_ROWS


def _regroup_kernel(src_off, dst_off, npc, zdst, znpc, bdst, bnpc, src_ref, zero_ref, dst_ref, sem):
    n_chunks = src_off.shape[0]
    n_zero = zdst.shape[0]
    piece = SUBLANES

    def aligned(o):
        return o if isinstance(o, int) else pl.multiple_of(o, piece)

    def piece_copy(src, so, do):
        return pltpu.make_async_copy(src.at[pl.ds(aligned(so), piece)],
                                     dst_ref.at[pl.ds(aligned(do), piece)], sem)

    def wait_pieces(n):
        def one(k, c):
            piece_copy(zero_ref, 0, 0).wait()
            return c
        lax.fori_loop(0, n, one, 0)

    def run(count, issue_chunk, pieces_of):
        def step(i, c):
            issue_chunk(i)

            @pl.when(i >= COPY_WINDOW)
            def _():
                wait_pieces(pieces_of(i - COPY_WINDOW))
            return c
        lax.fori_loop(0, count, step, 0)

        def drain(i, c):
            wait_pieces(pieces_of(i))
            return c
        lax.fori_loop(max(count - COPY_WINDOW, 0), count, drain, 0)

    def issue_data(i):
        so, do = src_off[i], dst_off[i]

        def one(k, c):
            piece_copy(src_ref, so + k * piece, do + k * piece).start()
            return c
        lax.fori_loop(0, npc[i], one, 0)

    def issue_zero(i):
        do = zdst[i]

        def one(k, c):
            piece_copy(zero_ref, 0, do + k * piece).start()
            return c
        lax.fori_loop(0, znpc[i], one, 0)

    run(n_chunks, issue_data, lambda i: npc[i])
    run(n_zero, issue_zero, lambda i: znpc[i])

    def big_copy(k):
        return pltpu.make_async_copy(
            zero_ref, dst_ref.at[pl.ds(pl.multiple_of(bdst[0] + k * BIG_PIECE, BIG_PIECE), BIG_PIECE)], sem)

    def big_start(k, c):
        big_copy(k).start()
        return c

    def big_wait(k, c):
        big_copy(k).wait()
        return c

    lax.fori_loop(0, bnpc[0], big_start, 0)
    lax.fori_loop(0, bnpc[0], big_wait, 0)


def _regroup(src_off, dst_off, npc, zdst, znpc, bdst, bnpc, src, zeros, out_rows, name):
    any_spec = pl.BlockSpec(memory_space=pl.ANY)
    return pl.pallas_call(
        _regroup_kernel,
        grid_spec=pltpu.PrefetchScalarGridSpec(
            num_scalar_prefetch=7, grid=(1,), in_specs=[any_spec, any_spec], out_specs=any_spec,
            scratch_shapes=[pltpu.SemaphoreType.DMA(())]),
        out_shape=jax.ShapeDtypeStruct((out_rows, HALF), I32),
        compiler_params=_params(("arbitrary",)),
        name=name,
    )(src_off, dst_off, npc, zdst, znpc, bdst, bnpc, src, zeros)


def _ffn_kernel(be_ref, nv_ref, x_ref, wg_ref, wu_ref, wd_ref, y_ref):
    del be_ref

    @pl.when(pl.program_id(0) < nv_ref[0])
    def _():
        x_lo, x_hi = _unpack_pairs(x_ref[...])
        wg = wg_ref[0].astype(BF16)
        wu = wu_ref[0].astype(BF16)
        a = _dot(x_lo, wg[:HALF]) + _dot(x_hi, wg[HALF:])
        b = _dot(x_lo, wu[:HALF]) + _dot(x_hi, wu[HALF:])
        y_ref[...] = _pack_pairs(_dot((_silu(a) * b).astype(BF16), wd_ref[0].astype(BF16)))

    @pl.when(pl.program_id(0) >= nv_ref[0])
    def _():
        y_ref[...] = jnp.zeros_like(y_ref)


def _ffn(blk_e, n_valid, xb, wg, wu, wd):
    rows = xb.shape[0]
    bk = MOE_BLOCK_ROWS
    xmap = lambda i, be, nv: (jnp.minimum(i, nv[0] - 1), 0)
    ymap = lambda i, be, nv: (i, 0)
    wmap = lambda i, be, nv: (be[i], 0, 0)
    return pl.pallas_call(
        _ffn_kernel,
        grid_spec=pltpu.PrefetchScalarGridSpec(
            num_scalar_prefetch=2,
            grid=(rows // bk,),
            in_specs=[pl.BlockSpec((bk, HALF), xmap),
                      pl.BlockSpec((1, D_MODEL, D_EXPERT), wmap),
                      pl.BlockSpec((1, D_MODEL, D_EXPERT), wmap),
                      pl.BlockSpec((1, D_EXPERT, D_MODEL), wmap)],
            out_specs=pl.BlockSpec((bk, HALF), ymap)),
        out_shape=jax.ShapeDtypeStruct((rows, HALF), I32),
        compiler_params=_params(("arbitrary",)),
        name="expert_ffn",
    )(blk_e, n_valid, xb, wg, wu, wd)


def _combine_kernel(meta_ref, h_ref, p_ref, yl_ref, gp_ref, wpg_ref, wpp_ref, gf_ref, o_ref):
    tm = h_ref.shape[0]
    m = meta_ref[...]
    ci = lax.broadcasted_iota(I32, (tm, LOCAL_ROWS), 1).astype(F32)
    sel = (jnp.where(ci == m[:, META_P0:META_P0 + 1], m[:, META_W0:META_W0 + 1], 0.0)
           + jnp.where(ci == m[:, META_P1:META_P1 + 1], m[:, META_W1:META_W1 + 1], 0.0)).astype(BF16)
    y_lo, y_hi = _unpack_pairs(yl_ref[...])
    moe = jnp.concatenate([_dot(sel, y_lo), _dot(sel, y_hi)], axis=1)
    h = h_ref[...] + moe
    gate = _sigmoid(_dot(_rms(h, gp_ref[...]).astype(BF16), wpg_ref[...]))
    h = h + _dot(p_ref[...].astype(BF16), wpp_ref[...]) * gate
    o_ref[...] = _rms(h, gf_ref[...])


def _combine(meta, h1, p2, yl, g_ple, w_pg, w_pp, g_fin):
    t = h1.shape[0]
    tm = ROW_TILE
    row = lambda n: pl.BlockSpec((tm, n), lambda i: (i, 0))
    return pl.pallas_call(
        _combine_kernel,
        grid=(t // tm,),
        in_specs=[row(LANES), row(D_MODEL), row(PLE_DIM),
                  pl.BlockSpec((LOCAL_ROWS, HALF), lambda i: (i, 0)),
                  _const_spec((1, D_MODEL)), _const_spec((D_MODEL, D_MODEL)),
                  _const_spec((PLE_DIM, D_MODEL)), _const_spec((1, D_MODEL))],
        out_specs=row(D_MODEL),
        out_shape=jax.ShapeDtypeStruct((t, D_MODEL), F32),
        compiler_params=_params(("parallel",)),
        name="combine",
    )(meta, h1, p2, yl, g_ple, w_pg, w_pp, g_fin)


def _row(v, n=None):
    v = v.astype(F32).reshape(1, -1)
    if n is not None and v.shape[1] < n:
        v = jnp.pad(v, ((0, 0), (0, n - v.shape[1])))
    return v


def _layer(h2, p2, batch, seq, norm_mix_g, w_in, ssd_conv_w, ssd_conv_b, ssd_dt_bias, ssd_a_log, ssd_d,
           ssd_norm_g, w_ssd_out, conf_dw_w, conf_dw_b, conf_ln_g, conf_ln_b, w_conf_out, w_o,
           norm_ffn_g, router_group_w, router_group_b, router_expert_w, router_expert_b,
           expert_w_gate, expert_w_up, expert_w_down, norm_ple_g, w_ple_gate, w_ple_proj, final_g):
    t = batch * seq
    tm = ROW_TILE
    ntile = t // tm
    w_main = jnp.concatenate([w_in[:, :OFF_DT], w_in[:, OFF_GLU:]], axis=1).astype(BF16)
    w_dt = jnp.pad(w_in[:, OFF_DT:OFF_GLU], ((0, 0), (0, LANES - SSD_HEADS))).astype(BF16)
    conv_tab = jnp.concatenate([ssd_conv_w.astype(F32), ssd_conv_b.astype(F32)[None, :]], axis=0)
    conv_tab = jnp.pad(conv_tab, ((0, SUBLANES - conv_tab.shape[0]), (0, 0)))
    conv_tab = conv_tab.reshape(SUBLANES, XBC_SLABS, LANES).transpose(1, 0, 2)
    d_full = jnp.repeat(ssd_d.astype(F32), SSD_HEADDIM).reshape(1, D_MODEL)
    head_of_lane = jnp.arange(D_MODEL) // SSD_HEADDIM
    expand = (jnp.arange(LANES)[:, None] == head_of_lane[None, :]).astype(BF16)
    dw3 = jnp.pad(conf_dw_w.astype(F32), ((0, CONF_HALO - CONF_KERNEL), (0, 0)))
    dw3 = dw3.reshape(CONF_HALO, N_SLABS, LANES).transpose(1, 0, 2)
    wr = jnp.concatenate([router_expert_w, router_group_w], axis=1).astype(F32)
    wr = jnp.pad(wr, ((0, 0), (0, LANES - wr.shape[1])))
    wr_hi = wr.astype(BF16)
    wr_lo = (wr - wr_hi.astype(F32)).astype(BF16)
    rb = _row(jnp.concatenate([router_expert_b, router_group_b]), LANES)

    z, xbc, dt, glu, gs, gc = _inproj(h2, _row(norm_mix_g), w_main, w_dt, _row(ssd_dt_bias, LANES))
    ys = _ssd(z, xbc, dt, gs, conv_tab, _row(ssd_a_log, LANES), d_full, expand,
              _row(ssd_norm_g), w_ssd_out.astype(BF16), batch, seq)
    h1, su, meta, tab = _merge(glu, ys, gc, h2, dw3, _row(conf_dw_b), _row(conf_ln_g), _row(conf_ln_b),
                               w_conf_out.astype(BF16), w_o.astype(BF16), _row(norm_ffn_g), wr_hi, wr_lo, rb,
                               batch, seq)

    bk = MOE_BLOCK_ROWS
    n_blocks = (2 * t + ntile * N_EXPERTS * (SUBLANES - 1)) // bk + N_EXPERTS + 1
    total_rows = n_blocks * bk
    rows8 = tab[:, TAB_ROWS, :N_EXPERTS].astype(I32)
    start = tab[:, TAB_START, :N_EXPERTS].astype(I32)
    per_expert = jnp.sum(rows8, axis=0)
    region = (per_expert + bk - 1) // bk * bk
    region_end = jnp.cumsum(region)
    region_start = region_end - region
    n_valid = (region_end[-1] // bk).astype(I32)
    local_off = (jnp.arange(ntile, dtype=I32) * LOCAL_ROWS)[:, None] + start
    global_off = region_start[None, :] + jnp.cumsum(rows8, axis=0) - rows8
    npc = (rows8 // SUBLANES).reshape(-1)
    local_off = local_off.reshape(-1)
    global_off = global_off.reshape(-1).astype(I32)
    blk = jnp.arange(n_blocks, dtype=I32)
    blk_e = jnp.sum((blk[:, None] * bk >= region_end[None, :]).astype(I32), axis=1)
    last_e = jnp.sum((((n_valid - 1) * bk) >= region_end).astype(I32))
    blk_e = jnp.minimum(jnp.where(blk < n_valid, blk_e, last_e), N_EXPERTS - 1).astype(I32)
    zeros = jnp.zeros((BIG_PIECE, HALF), I32)
    one = lambda v: jnp.reshape(v, (1,)).astype(I32)

    xb = _regroup(local_off, global_off, npc,
                  (region_start + per_expert).astype(I32), ((region - per_expert) // SUBLANES).astype(I32),
                  one(region_end[-1]), one(n_blocks - n_valid), su, zeros, total_rows, "regroup_in")
    yb = _ffn(blk_e, one(n_valid), xb, expert_w_gate, expert_w_up, expert_w_down)
    used = jnp.sum(rows8, axis=1)
    yl = _regroup(global_off, local_off, npc,
                  (jnp.arange(ntile, dtype=I32) * LOCAL_ROWS + used).astype(I32),
                  ((LOCAL_ROWS - used) // SUBLANES).astype(I32),
                  one(0), one(0), yb, zeros, ntile * LOCAL_ROWS, "regroup_out")
    return _combine(meta, h1, p2, yl, _row(norm_ple_g), w_ple_gate.astype(BF16),
                    w_ple_proj.astype(BF16), final_g)


def kernel(x, p, norm_mix_g, w_in, ssd_conv_w, ssd_conv_b, ssd_dt_bias, ssd_a_log, ssd_d, ssd_norm_g,
           w_ssd_out, conf_dw_w, conf_dw_b, conf_ln_g, conf_ln_b, w_conf_out, w_o, norm_ffn_g,
           router_group_w, router_group_b, router_expert_w, router_expert_b, expert_w_gate, expert_w_up,
           expert_w_down, norm_ple_g, w_ple_gate, w_ple_proj, final_norm_g):
    batch, seq, d = x.shape
    depth = p.shape[0]
    assert d == D_MODEL and depth == 1, "single-layer block with D_MODEL features"
    assert seq % ROW_TILE == 0 and ROW_TILE % SSD_CHUNK == 0
    h2 = x.reshape(batch * seq, d)
    out = _layer(h2, p[0].reshape(batch * seq, PLE_DIM), batch, seq, norm_mix_g[0], w_in[0], ssd_conv_w[0],
                 ssd_conv_b[0], ssd_dt_bias[0], ssd_a_log[0], ssd_d[0], ssd_norm_g[0], w_ssd_out[0],
                 conf_dw_w[0], conf_dw_b[0], conf_ln_g[0], conf_ln_b[0], w_conf_out[0], w_o[0],
                 norm_ffn_g[0], router_group_w[0], router_group_b[0], router_expert_w[0],
                 router_expert_b[0], expert_w_gate[0], expert_w_up[0], expert_w_down[0], norm_ple_g[0],
                 w_ple_gate[0], w_ple_proj[0], _row(final_norm_g))
    return out.reshape(batch, seq, d)
```

```python
import jax
import jax.numpy as jnp
from jax import lax
from jax.experimental import pallas as pl
from jax.experimental.pallas import tpu as pltpu

F32 = jnp.float32
BF16 = jnp.bfloat16
I32 = jnp.int32
EPS = 1e-6

D_MODEL = 1024
PLE_DIM = 256
SSD_HEADDIM = 64
SSD_HEADS = 16
SSD_GROUPS = 2
SSD_STATE = 128
SSD_CONV = 4
SSD_CHUNK = 128
SSD_BC = SSD_GROUPS * SSD_STATE
SSD_CONV_CH = D_MODEL + 2 * SSD_BC
CONF_KERNEL = 31
N_GROUPS = 4
EXPERTS_PER_GROUP = 8
N_EXPERTS = 32
D_EXPERT = 512

OFF_XBC = D_MODEL
OFF_DT = OFF_XBC + SSD_CONV_CH
OFF_GLU = OFF_DT + SSD_HEADS
OFF_GATE = OFF_GLU + 2 * D_MODEL

LANES = 128
SUBLANES = 8
VMEM_LIMIT_BYTES = 56 * 1024 * 1024

ROW_TILE = 512
MOE_BLOCK_ROWS = 512
NEG_BIG = -1e30
HEADS_PER_SLAB = LANES // SSD_HEADDIM
N_SLABS = D_MODEL // LANES
XBC_SLABS = SSD_CONV_CH // LANES
CONF_HALO = 32
HALF = D_MODEL // 2
LOCAL_ROWS = 2 * ROW_TILE + N_EXPERTS * SUBLANES
HIGH_MASK = -65536


def _sigmoid(v):
    return jax.nn.sigmoid(v)


def _silu(v):
    return v * jax.nn.sigmoid(v)


def _rms(v, g):
    return v * lax.rsqrt(jnp.mean(v * v, axis=-1, keepdims=True) + EPS) * g


def _dot(a, b):
    return jnp.dot(a, b, preferred_element_type=F32)


def _split_bf16(v, parts):
    out = []
    r = v
    for _ in range(parts):
        h = r.astype(BF16)
        out.append(h)
        r = r - h.astype(F32)
    return out


def _pack_pairs(v):
    r = v.astype(BF16).astype(F32)
    lo_bits = lax.bitcast_convert_type(r[:, :HALF], I32)
    lo = lax.shift_right_logical(lo_bits, jnp.full_like(lo_bits, 16))
    hi = lax.bitcast_convert_type(r[:, HALF:], I32) & HIGH_MASK
    return hi | lo


def _unpack_pairs(w):
    lo = lax.bitcast_convert_type(lax.shift_left(w, jnp.full_like(w, 16)), F32)
    hi = lax.bitcast_convert_type(w & HIGH_MASK, F32)
    return lo.astype(BF16), hi.astype(BF16)


def _const_spec(shape):
    nd = len(shape)
    return pl.BlockSpec(shape, lambda *_: (0,) * nd, pipeline_mode=pl.Buffered(1))


def _params(sem):
    return pltpu.CompilerParams(dimension_semantics=sem, vmem_limit_bytes=VMEM_LIMIT_BYTES)


def _inproj_kernel(x_ref, g_ref, w_ref, wdt_ref, dtb_ref,
                   z_ref, xbc_ref, dt_ref, glu_ref, gs_ref, gc_ref, u_scr):
    u_scr[...] = _rms(x_ref[...], g_ref[...]).astype(BF16)

    def seg(lo, n):
        return _dot(u_scr[...], w_ref[:, lo:lo + n])

    z_ref[...] = seg(0, D_MODEL).astype(BF16)
    blk = 512
    for j in range(SSD_CONV_CH // blk):
        xbc_ref[:, j * blk:(j + 1) * blk] = seg(D_MODEL + j * blk, blk).astype(BF16)
    o = D_MODEL + SSD_CONV_CH
    glu_ref[...] = (seg(o, D_MODEL) * _sigmoid(seg(o + D_MODEL, D_MODEL))).astype(BF16)
    gs_ref[...] = _sigmoid(seg(o + 2 * D_MODEL, D_MODEL)).astype(BF16)
    gc_ref[...] = _sigmoid(seg(o + 3 * D_MODEL, D_MODEL)).astype(BF16)
    dt_ref[...] = jax.nn.softplus(_dot(u_scr[...], wdt_ref[...]) + dtb_ref[...])


def _inproj(x2, g, w_main, w_dt, dt_bias):
    t = x2.shape[0]
    tm = ROW_TILE
    n_main = w_main.shape[1]
    row = lambda n: pl.BlockSpec((tm, n), lambda i: (i, 0))
    outs = [(D_MODEL, BF16), (SSD_CONV_CH, BF16), (LANES, F32), (D_MODEL, BF16), (D_MODEL, BF16), (D_MODEL, BF16)]
    return pl.pallas_call(
        _inproj_kernel,
        grid=(t // tm,),
        in_specs=[row(D_MODEL), _const_spec((1, D_MODEL)), _const_spec((D_MODEL, n_main)),
                  _const_spec((D_MODEL, LANES)), _const_spec((1, LANES))],
        out_specs=[row(n) for n, _ in outs],
        out_shape=[jax.ShapeDtypeStruct((t, n), d) for n, d in outs],
        scratch_shapes=[pltpu.VMEM((tm, D_MODEL), BF16)],
        compiler_params=_params(("parallel",)),
        name="inproj",
    )(x2, g, w_main, w_dt, dt_bias)


CONV_BIAS_ROW = SSD_CONV


def _ssd_kernel(z_ref, xbc_ref, dt_ref, gs_ref, cw_ref, alog_ref, dfull_ref, expand_ref,
                ng_ref, wout_ref, ys_ref, ext_scr, act_scr, state_scr, y_scr):
    tc = z_ref.shape[0]
    q = SSD_CHUNK

    @pl.when(pl.program_id(1) == 0)
    def _():
        ext_scr[:, 0:SUBLANES, :] = jnp.zeros((XBC_SLABS, SUBLANES, LANES), F32)
        state_scr[...] = jnp.zeros_like(state_scr)

    for j in range(XBC_SLABS):
        ext_scr[j, SUBLANES:SUBLANES + tc, :] = xbc_ref[:, j * LANES:(j + 1) * LANES].astype(F32)
    first = SUBLANES - (SSD_CONV - 1)

    def conv_slab(j, carry):
        for r in range(tc // q):
            acc = jnp.broadcast_to(cw_ref[j, CONV_BIAS_ROW:CONV_BIAS_ROW + 1, :], (q, LANES))
            for k in range(SSD_CONV):
                acc = acc + cw_ref[j, k:k + 1, :] * ext_scr[j, pl.ds(r * q + first + k, q), :]
            act_scr[j, r * q:(r + 1) * q, :] = _silu(acc)
        return carry

    lax.fori_loop(0, XBC_SLABS, conv_slab, 0)
    for j in range(XBC_SLABS):
        ext_scr[j, 0:SUBLANES, :] = ext_scr[j, tc:tc + SUBLANES, :]

    rowi = lax.broadcasted_iota(I32, (q, q), 0)
    coli = lax.broadcasted_iota(I32, (q, q), 1)
    causal = coli <= rowi
    tril = causal.astype(BF16)
    left = coli < SSD_HEADDIM
    a_neg = -jnp.exp(alog_ref[...])

    def chunk(c, carry):
        r0 = pl.multiple_of(c * q, q)
        rows = pl.ds(r0, q)
        dt = dt_ref[rows, :]
        parts = _split_bf16(dt * a_neg, 3)
        a_cs = _dot(tril, parts[0]) + _dot(tril, parts[1]) + _dot(tril, parts[2])
        a_cs_t = a_cs.T
        dt_t = dt.T
        w_t = jnp.exp(a_cs_t[:, q - 1:q] - a_cs_t) * dt_t
        cd8 = jnp.broadcast_to(jnp.exp(a_cs[q - 1:q, :]), (SUBLANES, LANES))
        cdp = _split_bf16(cd8, 3)
        cd_full = (_dot(cdp[0], expand_ref[...]) + _dot(cdp[1], expand_ref[...])
                   + _dot(cdp[2], expand_ref[...]))[0:1, :]

        slabs_per_group = N_SLABS // SSD_GROUPS
        for g in range(SSD_GROUPS):
            bg = act_scr[N_SLABS + g, rows, :]
            cg = act_scr[N_SLABS + SSD_GROUPS + g, rows, :]
            cb = lax.dot_general(cg.astype(BF16), bg.astype(BF16), (((1,), (1,)), ((), ())),
                                 preferred_element_type=F32)
            bg_t = bg.T
            for jj in range(slabs_per_group):
                j = g * slabs_per_group + jj
                ls = slice(j * LANES, (j + 1) * LANES)
                m_parts, c_parts, b_parts = [], [], []
                for hh in range(HEADS_PER_SLAB):
                    h = j * HEADS_PER_SLAB + hh
                    col = jnp.broadcast_to(a_cs[:, h:h + 1], (q, q))
                    seg = col - a_cs_t[h:h + 1, :]
                    dec = jnp.exp(jnp.where(causal, seg, NEG_BIG))
                    m_parts.append((cb * dec * dt_t[h:h + 1, :]).astype(BF16))
                    c_parts.append((cg * jnp.exp(col)).astype(BF16))
                    b_parts.append((bg_t * w_t[h:h + 1, :]).astype(BF16))
                xs = act_scr[j, rows, :]
                xs_b = xs.astype(BF16)
                st = state_scr[:, ls]
                st_b = st.astype(BF16)
                zero = jnp.zeros_like(xs_b)
                rhs_x = jnp.concatenate([jnp.where(left, xs_b, zero), jnp.where(left, zero, xs_b)], axis=0)
                rhs_s = jnp.concatenate([jnp.where(left, st_b, zero), jnp.where(left, zero, st_b)], axis=0)
                y = (_dot(jnp.concatenate(m_parts, axis=1), rhs_x)
                     + _dot(jnp.concatenate(c_parts, axis=1), rhs_s)
                     + dfull_ref[:, ls] * xs)
                y_scr[rows, ls] = y
                state_scr[:, ls] = st * cd_full[:, ls] + _dot(jnp.concatenate(b_parts, axis=1), rhs_x)
        return carry

    lax.fori_loop(0, tc // q, chunk, 0)

    v = y_scr[...] * _silu(z_ref[...].astype(F32))
    v = _rms(v, ng_ref[...])
    ys_ref[...] = (_dot(v.astype(BF16), wout_ref[...]) * gs_ref[...].astype(F32)).astype(BF16)


def _ssd(z, xbc, dt, gs, conv_tab, a_log, d_full, expand, norm_g, w_out, batch, seq):
    tc = ROW_TILE
    nst = seq // tc
    row = lambda n: pl.BlockSpec((tc, n), lambda b, s: (b * nst + s, 0))
    return pl.pallas_call(
        _ssd_kernel,
        grid=(batch, nst),
        in_specs=[row(D_MODEL), row(SSD_CONV_CH), row(LANES), row(D_MODEL),
                  _const_spec((XBC_SLABS, SUBLANES, LANES)),
                  _const_spec((1, LANES)), _const_spec((1, D_MODEL)), _const_spec((LANES, D_MODEL)),
                  _const_spec((1, D_MODEL)), _const_spec((D_MODEL, D_MODEL))],
        out_specs=row(D_MODEL),
        out_shape=jax.ShapeDtypeStruct((batch * seq, D_MODEL), BF16),
        scratch_shapes=[pltpu.VMEM((XBC_SLABS, SUBLANES + tc, LANES), F32),
                        pltpu.VMEM((XBC_SLABS, tc, LANES), F32),
                        pltpu.VMEM((SSD_STATE, D_MODEL), F32),
                        pltpu.VMEM((tc, D_MODEL), F32)],
        compiler_params=_params(("arbitrary", "arbitrary")),
        name="ssd",
    )(z, xbc, dt, gs, conv_tab, a_log, d_full, expand, norm_g, w_out)


META_P0, META_P1, META_W0, META_W1 = 0, 1, 2, 3
TAB_ROWS, TAB_START = 0, 1


def _route_and_sort(u, lg, su_ref, meta_ref, tab_ref):
    tm = u.shape[0]
    lane = lax.broadcasted_iota(I32, (tm, LANES), 1)
    far = jnp.int32(4 * LANES)

    def first_argmax(v):
        m = jnp.max(v, axis=-1, keepdims=True)
        return m, jnp.min(jnp.where(v == m, lane, far), axis=-1, keepdims=True)

    is_group = (lane >= N_EXPERTS) & (lane < N_EXPERTS + N_GROUPS)
    gl = jnp.where(is_group, lg, NEG_BIG)
    gmax, glane = first_argmax(gl)
    gidx = glane - N_EXPERTS
    g_prob = 1.0 / jnp.sum(jnp.where(is_group, jnp.exp(gl - gmax), 0.0), axis=-1, keepdims=True)

    in_group = (lane < N_EXPERTS) & ((lane // EXPERTS_PER_GROUP) == gidx)
    el = jnp.where(in_group, lg, NEG_BIG)
    m0, e0 = first_argmax(el)
    m1, e1 = first_argmax(jnp.where(lane == e0, NEG_BIG, el))
    ex = jnp.exp(m1 - m0)
    w0 = g_prob / (1.0 + ex)
    w1 = w0 * ex

    a0 = (lane == e0).astype(F32)
    a1 = (lane == e1).astype(F32)
    c0 = jnp.sum(a0, axis=0, keepdims=True)
    c1 = jnp.sum(a1, axis=0, keepdims=True)
    rows8 = jnp.floor((c0 + c1 + (SUBLANES - 1)) * (1.0 / SUBLANES)) * SUBLANES
    ei = lax.broadcasted_iota(I32, (LANES, LANES), 0)
    ej = lax.broadcasted_iota(I32, (LANES, LANES), 1)
    earlier_expert = (ei < ej).astype(BF16)
    start = _dot(jnp.broadcast_to(rows8, (SUBLANES, LANES)).astype(BF16), earlier_expert)[0:1, :]
    ti = lax.broadcasted_iota(I32, (tm, tm), 0)
    tj = lax.broadcasted_iota(I32, (tm, tm), 1)
    before = (tj < ti).astype(BF16)
    rank = _dot(before, jnp.concatenate([a0, a1], axis=1).astype(BF16))
    p0 = jnp.sum(a0 * (rank[:, :LANES] + start), axis=-1, keepdims=True)
    p1 = jnp.sum(a1 * (rank[:, LANES:] + start + c0), axis=-1, keepdims=True)

    meta = jnp.zeros((tm, LANES), F32)
    for pos, val in ((META_P0, p0), (META_P1, p1), (META_W0, w0), (META_W1, w1)):
        meta = jnp.where(lane == pos, val, meta)
    meta_ref[...] = meta
    sub = lax.broadcasted_iota(I32, (SUBLANES, LANES), 0)
    tab_ref[0] = jnp.where(sub == TAB_ROWS, rows8, jnp.where(sub == TAB_START, start, 0.0))

    meta_t = meta.T
    ri = lax.broadcasted_iota(I32, (LOCAL_ROWS, tm), 0).astype(F32)
    take = (ri == meta_t[META_P0:META_P0 + 1, :]) | (ri == meta_t[META_P1:META_P1 + 1, :])
    gathered = _dot(jnp.where(take, 1.0, 0.0).astype(BF16), u.astype(BF16))
    su_ref[...] = _pack_pairs(gathered)


def _merge_kernel(glu_ref, ys_ref, gc_ref, x_ref, dw_ref, db_ref, lng_ref, lnb_ref, wc_ref, wo_ref,
                  nf_ref, wrh_ref, wrl_ref, rb_ref, h_ref, su_ref, meta_ref, tab_ref, ext_scr, c_scr):
    tm = x_ref.shape[0]

    @pl.when(pl.program_id(1) == 0)
    def _():
        ext_scr[:, 0:CONF_HALO, :] = jnp.zeros((N_SLABS, CONF_HALO, LANES), F32)

    for j in range(N_SLABS):
        ext_scr[j, CONF_HALO:CONF_HALO + tm, :] = glu_ref[:, j * LANES:(j + 1) * LANES].astype(F32)

    rc = 128
    first = CONF_HALO - (CONF_KERNEL - 1)

    def slab(j, carry):
        for r in range(tm // rc):
            acc = jnp.zeros((rc, LANES), F32)
            for k in range(CONF_KERNEL):
                acc = acc + dw_ref[j, k:k + 1, :] * ext_scr[j, pl.ds(r * rc + first + k, rc), :]
            c_scr[j, r * rc:(r + 1) * rc, :] = acc
        return carry

    lax.fori_loop(0, N_SLABS, slab, 0)
    for j in range(N_SLABS):
        ext_scr[j, 0:CONF_HALO, :] = ext_scr[j, tm:tm + CONF_HALO, :]

    c = jnp.concatenate([c_scr[j] for j in range(N_SLABS)], axis=1) + db_ref[...]
    mu = jnp.mean(c, axis=-1, keepdims=True)
    xc = c - mu
    yln = xc * lax.rsqrt(jnp.mean(xc * xc, axis=-1, keepdims=True) + EPS) * lng_ref[...] + lnb_ref[...]
    y_conv = _dot(_silu(yln).astype(BF16), wc_ref[...])
    mix = gc_ref[...].astype(F32) * y_conv + ys_ref[...].astype(F32)
    h = x_ref[...] + _dot(mix.astype(BF16), wo_ref[...])
    h_ref[...] = h
    u = _rms(h, nf_ref[...])
    u_hi, u_lo = _split_bf16(u, 2)
    lg = _dot(u_hi, wrh_ref[...]) + _dot(u_hi, wrl_ref[...]) + _dot(u_lo, wrh_ref[...]) + rb_ref[...]
    _route_and_sort(u, lg, su_ref, meta_ref, tab_ref)


def _merge(glu, ys, gc, x2, dw3, dw_b, ln_g, ln_b, w_conf, w_o, nf_g, wr_hi, wr_lo, rb, batch, seq):
    tm = ROW_TILE
    nst = seq // tm
    t = batch * seq
    ntile = t // tm
    tile = lambda b, s: b * nst + s
    row = lambda n: pl.BlockSpec((tm, n), lambda b, s: (tile(b, s), 0))
    return pl.pallas_call(
        _merge_kernel,
        grid=(batch, nst),
        in_specs=[row(D_MODEL), row(D_MODEL), row(D_MODEL), row(D_MODEL),
                  _const_spec((N_SLABS, CONF_HALO, LANES)), _const_spec((1, D_MODEL)),
                  _const_spec((1, D_MODEL)), _const_spec((1, D_MODEL)),
                  _const_spec((D_MODEL, D_MODEL)), _const_spec((D_MODEL, D_MODEL)),
                  _const_spec((1, D_MODEL)), _const_spec((D_MODEL, LANES)), _const_spec((D_MODEL, LANES)),
                  _const_spec((1, LANES))],
        out_specs=[row(D_MODEL),
                   pl.BlockSpec((LOCAL_ROWS, HALF), lambda b, s: (tile(b, s), 0)),
                   row(LANES),
                   pl.BlockSpec((1, SUBLANES, LANES), lambda b, s: (tile(b, s), 0, 0))],
        out_shape=[jax.ShapeDtypeStruct((t, D_MODEL), F32),
                   jax.ShapeDtypeStruct((ntile * LOCAL_ROWS, HALF), I32),
                   jax.ShapeDtypeStruct((t, LANES), F32),
                   jax.ShapeDtypeStruct((ntile, SUBLANES, LANES), F32)],
        scratch_shapes=[pltpu.VMEM((N_SLABS, CONF_HALO + tm, LANES), F32),
                        pltpu.VMEM((N_SLABS, tm, LANES), F32)],
        compiler_params=_params(("arbitrary", "arbitrary")),
        name="merge",
    )(glu, ys, gc, x2, dw3, dw_b, ln_g, ln_b, w_conf, w_o, nf_g, wr_hi, wr_lo, rb)


BIG_PIECE = MOE_BLOCK_ROWS


def _piece_loops(n_pieces_of, copy_of):
    def run(action):
        def chunk(e, c):
            def one(k, cc):
                action(copy_of(e, k))
                return cc
            lax.fori_loop(0, n_pieces_of(e), one, 0)
            return c
        lax.fori_loop(0, N_EXPERTS, chunk, 0)
    return run


def _regroup_kernel(start, dst_off, npc, zdst, znpc, bdst, bnpc, su_ref, xb_ref, zero_scr, sem):
    i = pl.program_id(0)
    piece = SUBLANES

    def rows(o, n):
        return pl.ds(pl.multiple_of(o, n), n)

    @pl.when(i == 0)
    def _():
        zero_scr[...] = jnp.zeros_like(zero_scr)
        zero_pieces = _piece_loops(
            lambda e: znpc[e],
            lambda e, k: pltpu.make_async_copy(zero_scr.at[pl.ds(0, piece)],
                                               xb_ref.at[rows(zdst[e] + k * piece, piece)], sem))
        zero_pieces(lambda cp: cp.start())

        def big(k):
            return pltpu.make_async_copy(zero_scr, xb_ref.at[rows(bdst[0] + k * BIG_PIECE, BIG_PIECE)], sem)

        def big_start(k, c):
            big(k).start()
            return c

        def big_wait(k, c):
            big(k).wait()
            return c

        lax.fori_loop(0, bnpc[0], big_start, 0)
        zero_pieces(lambda cp: cp.wait())
        lax.fori_loop(0, bnpc[0], big_wait, 0)

    base = i * N_EXPERTS
    data_pieces = _piece_loops(
        lambda e: npc[base + e],
        lambda e, k: pltpu.make_async_copy(su_ref.at[rows(start[base + e] + k * piece, piece)],
                                           xb_ref.at[rows(dst_off[base + e] + k * piece, piece)], sem))
    data_pieces(lambda cp: cp.start())
    data_pieces(lambda cp: cp.wait())


def _regroup(start, dst_off, npc, zdst, znpc, bdst, bnpc, su, out_rows):
    ntile = su.shape[0] // LOCAL_ROWS
    return pl.pallas_call(
        _regroup_kernel,
        grid_spec=pltpu.PrefetchScalarGridSpec(
            num_scalar_prefetch=7, grid=(ntile,),
            in_specs=[pl.BlockSpec((LOCAL_ROWS, HALF), lambda i, *_: (i, 0))],
            out_specs=pl.BlockSpec(memory_space=pl.ANY),
            scratch_shapes=[pltpu.VMEM((BIG_PIECE, HALF), I32), pltpu.SemaphoreType.DMA(())]),
        out_shape=jax.ShapeDtypeStruct((out_rows, HALF), I32),
        compiler_params=_params(("arbitrary",)),
        name="regroup",
    )(start, dst_off, npc, zdst, znpc, bdst, bnpc, su)


def _ffn_kernel(be_ref, nv_ref, x_ref, wg_ref, wu_ref, wd_ref, y_ref):
    del be_ref

    @pl.when(pl.program_id(0) < nv_ref[0])
    def _():
        x_lo, x_hi = _unpack_pairs(x_ref[...])
        wg = wg_ref[0].astype(BF16)
        wu = wu_ref[0].astype(BF16)
        a = _dot(x_lo, wg[:HALF]) + _dot(x_hi, wg[HALF:])
        b = _dot(x_lo, wu[:HALF]) + _dot(x_hi, wu[HALF:])
        y_ref[...] = _pack_pairs(_dot((_silu(a) * b).astype(BF16), wd_ref[0].astype(BF16)))

    @pl.when(pl.program_id(0) >= nv_ref[0])
    def _():
        y_ref[...] = jnp.zeros_like(y_ref)


def _ffn(blk_e, n_valid, xb, wg, wu, wd):
    rows = xb.shape[0]
    bk = MOE_BLOCK_ROWS
    xmap = lambda i, be, nv: (jnp.minimum(i, nv[0] - 1), 0)
    ymap = lambda i, be, nv: (i, 0)
    wmap = lambda i, be, nv: (be[i], 0, 0)
    return pl.pallas_call(
        _ffn_kernel,
        grid_spec=pltpu.PrefetchScalarGridSpec(
            num_scalar_prefetch=2,
            grid=(rows // bk,),
            in_specs=[pl.BlockSpec((bk, HALF), xmap),
                      pl.BlockSpec((1, D_MODEL, D_EXPERT), wmap),
                      pl.BlockSpec((1, D_MODEL, D_EXPERT), wmap),
                      pl.BlockSpec((1, D_EXPERT, D_MODEL), wmap)],
            out_specs=pl.BlockSpec((bk, HALF), ymap)),
        out_shape=jax.ShapeDtypeStruct((rows, HALF), I32),
        compiler_params=_params(("arbitrary",)),
        name="expert_ffn",
    )(blk_e, n_valid, xb, wg, wu, wd)


def _combine_kernel(start, src_off, npc, used, meta_ref, h_ref, p_ref, yb_ref, gp_ref, wpg_ref, wpp_ref,
                    gf_ref, o_ref, yl_ref, sel_scr, ple_scr, sem):
    tm = h_ref.shape[0]
    i = pl.program_id(0)
    piece = SUBLANES

    def rows(o):
        return pl.ds(pl.multiple_of(o, piece), piece)

    base = i * N_EXPERTS
    pieces = _piece_loops(
        lambda e: npc[base + e],
        lambda e, k: pltpu.make_async_copy(yb_ref.at[rows(src_off[base + e] + k * piece)],
                                           yl_ref.at[rows(start[base + e] + k * piece)], sem))
    pieces(lambda cp: cp.start())

    def zero_tail(k, c):
        yl_ref[rows(used[i] + k * piece), :] = jnp.zeros((piece, HALF), I32)
        return c

    lax.fori_loop(0, (LOCAL_ROWS - used[i]) // piece, zero_tail, 0)

    m = meta_ref[...]
    ci = lax.broadcasted_iota(I32, (tm, LOCAL_ROWS), 1).astype(F32)
    sel_scr[...] = (jnp.where(ci == m[:, META_P0:META_P0 + 1], m[:, META_W0:META_W0 + 1], 0.0)
                    + jnp.where(ci == m[:, META_P1:META_P1 + 1], m[:, META_W1:META_W1 + 1], 0.0)
                    ).astype(BF16)
    ple_scr[...] = _dot(p_ref[...].astype(BF16), wpp_ref[...])
    pieces(lambda cp: cp.wait())

    y_lo, y_hi = _unpack_pairs(yl_ref[...])
    moe = jnp.concatenate([_dot(sel_scr[...], y_lo), _dot(sel_scr[...], y_hi)], axis=1)
    h = h_ref[...] + moe
    gate = _sigmoid(_dot(_rms(h, gp_ref[...]).astype(BF16), wpg_ref[...]))
    h = h + ple_scr[...] * gate
    o_ref[...] = _rms(h, gf_ref[...])


def _combine(start, src_off, npc, used, meta, h1, p2, yb, g_ple, w_pg, w_pp, g_fin):
    t = h1.shape[0]
    tm = ROW_TILE
    row = lambda n: pl.BlockSpec((tm, n), lambda i, *_: (i, 0))
    return pl.pallas_call(
        _combine_kernel,
        grid_spec=pltpu.PrefetchScalarGridSpec(
            num_scalar_prefetch=4, grid=(t // tm,),
            in_specs=[row(LANES), row(D_MODEL), row(PLE_DIM), pl.BlockSpec(memory_space=pl.ANY),
                      _const_spec((1, D_MODEL)), _const_spec((D_MODEL, D_MODEL)),
                      _const_spec((PLE_DIM, D_MODEL)), _const_spec((1, D_MODEL))],
            out_specs=row(D_MODEL),
            scratch_shapes=[pltpu.VMEM((LOCAL_ROWS, HALF), I32), pltpu.VMEM((tm, LOCAL_ROWS), BF16),
                            pltpu.VMEM((tm, D_MODEL), F32), pltpu.SemaphoreType.DMA(())]),
        out_shape=jax.ShapeDtypeStruct((t, D_MODEL), F32),
        compiler_params=_params(("arbitrary",)),
        name="combine",
    )(start, src_off, npc, used, meta, h1, p2, yb, g_ple, w_pg, w_pp, g_fin)


def _row(v, n=None):
    v = v.astype(F32).reshape(1, -1)
    if n is not None and v.shape[1] < n:
        v = jnp.pad(v, ((0, 0), (0, n - v.shape[1])))
    return v


def _layer(h2, p2, batch, seq, norm_mix_g, w_in, ssd_conv_w, ssd_conv_b, ssd_dt_bias, ssd_a_log, ssd_d,
           ssd_norm_g, w_ssd_out, conf_dw_w, conf_dw_b, conf_ln_g, conf_ln_b, w_conf_out, w_o,
           norm_ffn_g, router_group_w, router_group_b, router_expert_w, router_expert_b,
           expert_w_gate, expert_w_up, expert_w_down, norm_ple_g, w_ple_gate, w_ple_proj, final_g):
    t = batch * seq
    tm = ROW_TILE
    ntile = t // tm
    w_main = jnp.concatenate([w_in[:, :OFF_DT], w_in[:, OFF_GLU:]], axis=1).astype(BF16)
    w_dt = jnp.pad(w_in[:, OFF_DT:OFF_GLU], ((0, 0), (0, LANES - SSD_HEADS))).astype(BF16)
    conv_tab = jnp.concatenate([ssd_conv_w.astype(F32), ssd_conv_b.astype(F32)[None, :]], axis=0)
    conv_tab = jnp.pad(conv_tab, ((0, SUBLANES - conv_tab.shape[0]), (0, 0)))
    conv_tab = conv_tab.reshape(SUBLANES, XBC_SLABS, LANES).transpose(1, 0, 2)
    d_full = jnp.repeat(ssd_d.astype(F32), SSD_HEADDIM).reshape(1, D_MODEL)
    head_of_lane = jnp.arange(D_MODEL) // SSD_HEADDIM
    expand = (jnp.arange(LANES)[:, None] == head_of_lane[None, :]).astype(BF16)
    dw3 = jnp.pad(conf_dw_w.astype(F32), ((0, CONF_HALO - CONF_KERNEL), (0, 0)))
    dw3 = dw3.reshape(CONF_HALO, N_SLABS, LANES).transpose(1, 0, 2)
    wr = jnp.concatenate([router_expert_w, router_group_w], axis=1).astype(F32)
    wr = jnp.pad(wr, ((0, 0), (0, LANES - wr.shape[1])))
    wr_hi = wr.astype(BF16)
    wr_lo = (wr - wr_hi.astype(F32)).astype(BF16)
    rb = _row(jnp.concatenate([router_expert_b, router_group_b]), LANES)

    z, xbc, dt, glu, gs, gc = _inproj(h2, _row(norm_mix_g), w_main, w_dt, _row(ssd_dt_bias, LANES))
    ys = _ssd(z, xbc, dt, gs, conv_tab, _row(ssd_a_log, LANES), d_full, expand,
              _row(ssd_norm_g), w_ssd_out.astype(BF16), batch, seq)
    h1, su, meta, tab = _merge(glu, ys, gc, h2, dw3, _row(conf_dw_b), _row(conf_ln_g), _row(conf_ln_b),
                               w_conf_out.astype(BF16), w_o.astype(BF16), _row(norm_ffn_g), wr_hi, wr_lo, rb,
                               batch, seq)

    bk = MOE_BLOCK_ROWS
    n_blocks = (2 * t + ntile * N_EXPERTS * (SUBLANES - 1)) // bk + N_EXPERTS + 1
    total_rows = n_blocks * bk
    rows8 = tab[:, TAB_ROWS, :N_EXPERTS].astype(I32)
    start = tab[:, TAB_START, :N_EXPERTS].astype(I32)
    per_expert = jnp.sum(rows8, axis=0)
    region = (per_expert + bk - 1) // bk * bk
    region_end = jnp.cumsum(region)
    region_start = region_end - region
    n_valid = (region_end[-1] // bk).astype(I32)
    global_off = region_start[None, :] + jnp.cumsum(rows8, axis=0) - rows8
    npc = (rows8 // SUBLANES).reshape(-1)
    start = start.reshape(-1)
    global_off = global_off.reshape(-1).astype(I32)
    blk = jnp.arange(n_blocks, dtype=I32)
    blk_e = jnp.sum((blk[:, None] * bk >= region_end[None, :]).astype(I32), axis=1)
    last_e = jnp.sum((((n_valid - 1) * bk) >= region_end).astype(I32))
    blk_e = jnp.minimum(jnp.where(blk < n_valid, blk_e, last_e), N_EXPERTS - 1).astype(I32)
    one = lambda v: jnp.reshape(v, (1,)).astype(I32)

    xb = _regroup(start, global_off, npc,
                  (region_start + per_expert).astype(I32), ((region - per_expert) // SUBLANES).astype(I32),
                  one(region_end[-1]), one(n_blocks - n_valid), su, total_rows)
    yb = _ffn(blk_e, one(n_valid), xb, expert_w_gate, expert_w_up, expert_w_down)
    used = jnp.sum(rows8, axis=1).astype(I32)
    return _combine(start, global_off, npc, used, meta, h1, p2, yb, _row(norm_ple_g),
                    w_ple_gate.astype(BF16), w_ple_proj.astype(BF16), final_g)


def kernel(x, p, norm_mix_g, w_in, ssd_conv_w, ssd_conv_b, ssd_dt_bias, ssd_a_log, ssd_d, ssd_norm_g,
           w_ssd_out, conf_dw_w, conf_dw_b, conf_ln_g, conf_ln_b, w_conf_out, w_o, norm_ffn_g,
           router_group_w, router_group_b, router_expert_w, router_expert_b, expert_w_gate, expert_w_up,
           expert_w_down, norm_ple_g, w_ple_gate, w_ple_proj, final_norm_g):
    batch, seq, d = x.shape
    depth = p.shape[0]
    assert d == D_MODEL and depth == 1, "single-layer block with D_MODEL features"
    assert seq % ROW_TILE == 0 and ROW_TILE % SSD_CHUNK == 0
    h2 = x.reshape(batch * seq, d)
    out = _layer(h2, p[0].reshape(batch * seq, PLE_DIM), batch, seq, norm_mix_g[0], w_in[0], ssd_conv_w[0],
                 ssd_conv_b[0], ssd_dt_bias[0], ssd_a_log[0], ssd_d[0], ssd_norm_g[0], w_ssd_out[0],
                 conf_dw_w[0], conf_dw_b[0], conf_ln_g[0], conf_ln_b[0], w_conf_out[0], w_o[0],
                 norm_ffn_g[0], router_group_w[0], router_group_b[0], router_expert_w[0],
                 router_expert_b[0], expert_w_gate[0], expert_w_up[0], expert_w_down[0], norm_ple_g[0],
                 w_ple_gate[0], w_ple_proj[0], _row(final_norm_g))
    return out.reshape(batch, seq, d)
```

```python
import jax
import jax.numpy as jnp
from jax import lax
from jax.experimental import pallas as pl
from jax.experimental.pallas import tpu as pltpu

F32 = jnp.float32
BF16 = jnp.bfloat16
I32 = jnp.int32
EPS = 1e-6

D_MODEL = 1024
PLE_DIM = 256
SSD_HEADDIM = 64
SSD_HEADS = 16
SSD_GROUPS = 2
SSD_STATE = 128
SSD_CONV = 4
SSD_CHUNK = 128
SSD_BC = SSD_GROUPS * SSD_STATE
SSD_CONV_CH = D_MODEL + 2 * SSD_BC
CONF_KERNEL = 31
N_GROUPS = 4
EXPERTS_PER_GROUP = 8
N_EXPERTS = 32
D_EXPERT = 512

OFF_XBC = D_MODEL
OFF_DT = OFF_XBC + SSD_CONV_CH
OFF_GLU = OFF_DT + SSD_HEADS
OFF_GATE = OFF_GLU + 2 * D_MODEL

LANES = 128
SUBLANES = 8
VMEM_LIMIT_BYTES = 56 * 1024 * 1024

ROW_TILE = 512
MOE_BLOCK_ROWS = 512
NEG_BIG = -1e30
HEADS_PER_SLAB = LANES // SSD_HEADDIM
N_SLABS = D_MODEL // LANES
XBC_SLABS = SSD_CONV_CH // LANES
CONF_HALO = 32
HALF = D_MODEL // 2
LOCAL_ROWS = 2 * ROW_TILE + N_EXPERTS * SUBLANES
HIGH_MASK = -65536


def _sigmoid(v):
    return jax.nn.sigmoid(v)


def _silu(v):
    return v * jax.nn.sigmoid(v)


def _rms(v, g):
    return v * lax.rsqrt(jnp.mean(v * v, axis=-1, keepdims=True) + EPS) * g


def _dot(a, b):
    return jnp.dot(a, b, preferred_element_type=F32)


def _split_bf16(v, parts):
    out = []
    r = v
    for _ in range(parts):
        h = r.astype(BF16)
        out.append(h)
        r = r - h.astype(F32)
    return out


def _pack_pairs(v):
    r = v.astype(BF16).astype(F32)
    lo_bits = lax.bitcast_convert_type(r[:, :HALF], I32)
    lo = lax.shift_right_logical(lo_bits, jnp.full_like(lo_bits, 16))
    hi = lax.bitcast_convert_type(r[:, HALF:], I32) & HIGH_MASK
    return hi | lo


def _unpack_pairs(w):
    lo = lax.bitcast_convert_type(lax.shift_left(w, jnp.full_like(w, 16)), F32)
    hi = lax.bitcast_convert_type(w & HIGH_MASK, F32)
    return lo.astype(BF16), hi.astype(BF16)


def _const_spec(shape):
    nd = len(shape)
    return pl.BlockSpec(shape, lambda *_: (0,) * nd, pipeline_mode=pl.Buffered(1))


def _params(sem):
    return pltpu.CompilerParams(dimension_semantics=sem, vmem_limit_bytes=VMEM_LIMIT_BYTES)


def _inproj_kernel(x_ref, g_ref, w_ref, wdt_ref, dtb_ref,
                   z_ref, xbc_ref, dt_ref, glu_ref, gs_ref, gc_ref, u_scr):
    u_scr[...] = _rms(x_ref[...], g_ref[...]).astype(BF16)

    def seg(lo, n):
        return _dot(u_scr[...], w_ref[:, lo:lo + n])

    z_ref[...] = seg(0, D_MODEL).astype(BF16)
    blk = 512
    for j in range(SSD_CONV_CH // blk):
        xbc_ref[:, j * blk:(j + 1) * blk] = seg(D_MODEL + j * blk, blk).astype(BF16)
    o = D_MODEL + SSD_CONV_CH
    glu_ref[...] = (seg(o, D_MODEL) * _sigmoid(seg(o + D_MODEL, D_MODEL))).astype(BF16)
    gs_ref[...] = _sigmoid(seg(o + 2 * D_MODEL, D_MODEL)).astype(BF16)
    gc_ref[...] = _sigmoid(seg(o + 3 * D_MODEL, D_MODEL)).astype(BF16)
    dt_ref[...] = jax.nn.softplus(_dot(u_scr[...], wdt_ref[...]) + dtb_ref[...])


def _inproj(x2, g, w_main, w_dt, dt_bias):
    t = x2.shape[0]
    tm = ROW_TILE
    n_main = w_main.shape[1]
    row = lambda n: pl.BlockSpec((tm, n), lambda i: (i, 0))
    outs = [(D_MODEL, BF16), (SSD_CONV_CH, BF16), (LANES, F32), (D_MODEL, BF16), (D_MODEL, BF16), (D_MODEL, BF16)]
    return pl.pallas_call(
        _inproj_kernel,
        grid=(t // tm,),
        in_specs=[row(D_MODEL), _const_spec((1, D_MODEL)), _const_spec((D_MODEL, n_main)),
                  _const_spec((D_MODEL, LANES)), _const_spec((1, LANES))],
        out_specs=[row(n) for n, _ in outs],
        out_shape=[jax.ShapeDtypeStruct((t, n), d) for n, d in outs],
        scratch_shapes=[pltpu.VMEM((tm, D_MODEL), BF16)],
        compiler_params=_params(("parallel",)),
        name="inproj",
    )(x2, g, w_main, w_dt, dt_bias)


CONV_BIAS_ROW = SSD_CONV


def _ssd_kernel(z_ref, xbc_ref, dt_ref, gs_ref, cw_ref, alog_ref, dfull_ref, expand_ref,
                ng_ref, wout_ref, ys_ref, ext_scr, act_scr, state_scr, y_scr):
    tc = z_ref.shape[0]
    q = SSD_CHUNK

    @pl.when(pl.program_id(1) == 0)
    def _():
        ext_scr[:, 0:SUBLANES, :] = jnp.zeros((XBC_SLABS, SUBLANES, LANES), F32)
        state_scr[...] = jnp.zeros_like(state_scr)

    for j in range(XBC_SLABS):
        ext_scr[j, SUBLANES:SUBLANES + tc, :] = xbc_ref[:, j * LANES:(j + 1) * LANES].astype(F32)
    first = SUBLANES - (SSD_CONV - 1)

    def conv_slab(j, carry):
        for r in range(tc // q):
            acc = jnp.broadcast_to(cw_ref[j, CONV_BIAS_ROW:CONV_BIAS_ROW + 1, :], (q, LANES))
            for k in range(SSD_CONV):
                acc = acc + cw_ref[j, k:k + 1, :] * ext_scr[j, pl.ds(r * q + first + k, q), :]
            act_scr[j, r * q:(r + 1) * q, :] = _silu(acc)
        return carry

    lax.fori_loop(0, XBC_SLABS, conv_slab, 0)
    for j in range(XBC_SLABS):
        ext_scr[j, 0:SUBLANES, :] = ext_scr[j, tc:tc + SUBLANES, :]

    rowi = lax.broadcasted_iota(I32, (q, q), 0)
    coli = lax.broadcasted_iota(I32, (q, q), 1)
    causal = coli <= rowi
    tril = causal.astype(BF16)
    left = coli < SSD_HEADDIM
    a_neg = -jnp.exp(alog_ref[...])

    def chunk(c, carry):
        r0 = pl.multiple_of(c * q, q)
        rows = pl.ds(r0, q)
        dt = dt_ref[rows, :]
        parts = _split_bf16(dt * a_neg, 3)
        a_cs = _dot(tril, parts[0]) + _dot(tril, parts[1]) + _dot(tril, parts[2])
        a_cs_t = a_cs.T
        dt_t = dt.T
        w_t = jnp.exp(a_cs_t[:, q - 1:q] - a_cs_t) * dt_t
        cd8 = jnp.broadcast_to(jnp.exp(a_cs[q - 1:q, :]), (SUBLANES, LANES))
        cdp = _split_bf16(cd8, 3)
        cd_full = (_dot(cdp[0], expand_ref[...]) + _dot(cdp[1], expand_ref[...])
                   + _dot(cdp[2], expand_ref[...]))[0:1, :]

        slabs_per_group = N_SLABS // SSD_GROUPS
        for g in range(SSD_GROUPS):
            bg = act_scr[N_SLABS + g, rows, :]
            cg = act_scr[N_SLABS + SSD_GROUPS + g, rows, :]
            cb = lax.dot_general(cg.astype(BF16), bg.astype(BF16), (((1,), (1,)), ((), ())),
                                 preferred_element_type=F32)
            bg_t = bg.T
            for jj in range(slabs_per_group):
                j = g * slabs_per_group + jj
                ls = slice(j * LANES, (j + 1) * LANES)
                m_parts, c_parts, b_parts = [], [], []
                for hh in range(HEADS_PER_SLAB):
                    h = j * HEADS_PER_SLAB + hh
                    col = jnp.broadcast_to(a_cs[:, h:h + 1], (q, q))
                    seg = col - a_cs_t[h:h + 1, :]
                    dec = jnp.exp(jnp.where(causal, seg, NEG_BIG))
                    m_parts.append((cb * dec * dt_t[h:h + 1, :]).astype(BF16))
                    c_parts.append((cg * jnp.exp(col)).astype(BF16))
                    b_parts.append((bg_t * w_t[h:h + 1, :]).astype(BF16))
                xs = act_scr[j, rows, :]
                xs_b = xs.astype(BF16)
                st = state_scr[:, ls]
                st_b = st.astype(BF16)
                zero = jnp.zeros_like(xs_b)
                rhs_x = jnp.concatenate([jnp.where(left, xs_b, zero), jnp.where(left, zero, xs_b)], axis=0)
                rhs_s = jnp.concatenate([jnp.where(left, st_b, zero), jnp.where(left, zero, st_b)], axis=0)
                y = (_dot(jnp.concatenate(m_parts, axis=1), rhs_x)
                     + _dot(jnp.concatenate(c_parts, axis=1), rhs_s)
                     + dfull_ref[:, ls] * xs)
                y_scr[rows, ls] = y
                state_scr[:, ls] = st * cd_full[:, ls] + _dot(jnp.concatenate(b_parts, axis=1), rhs_x)
        return carry

    lax.fori_loop(0, tc // q, chunk, 0)

    v = y_scr[...] * _silu(z_ref[...].astype(F32))
    v = _rms(v, ng_ref[...])
    ys_ref[...] = (_dot(v.astype(BF16), wout_ref[...]) * gs_ref[...].astype(F32)).astype(BF16)


def _ssd(z, xbc, dt, gs, conv_tab, a_log, d_full, expand, norm_g, w_out, batch, seq):
    tc = ROW_TILE
    nst = seq // tc
    row = lambda n: pl.BlockSpec((tc, n), lambda b, s: (b * nst + s, 0))
    return pl.pallas_call(
        _ssd_kernel,
        grid=(batch, nst),
        in_specs=[row(D_MODEL), row(SSD_CONV_CH), row(LANES), row(D_MODEL),
                  _const_spec((XBC_SLABS, SUBLANES, LANES)),
                  _const_spec((1, LANES)), _const_spec((1, D_MODEL)), _const_spec((LANES, D_MODEL)),
                  _const_spec((1, D_MODEL)), _const_spec((D_MODEL, D_MODEL))],
        out_specs=row(D_MODEL),
        out_shape=jax.ShapeDtypeStruct((batch * seq, D_MODEL), BF16),
        scratch_shapes=[pltpu.VMEM((XBC_SLABS, SUBLANES + tc, LANES), F32),
                        pltpu.VMEM((XBC_SLABS, tc, LANES), F32),
                        pltpu.VMEM((SSD_STATE, D_MODEL), F32),
                        pltpu.VMEM((tc, D_MODEL), F32)],
        compiler_params=_params(("arbitrary", "arbitrary")),
        name="ssd",
    )(z, xbc, dt, gs, conv_tab, a_log, d_full, expand, norm_g, w_out)


META_P0, META_P1, META_W0, META_W1 = 0, 1, 2, 3
TAB_ROWS, TAB_START = 0, 1


def _route_and_sort(u, lg, su_ref, meta_ref, tab_ref):
    tm = u.shape[0]
    lane = lax.broadcasted_iota(I32, (tm, LANES), 1)
    far = jnp.int32(4 * LANES)

    def first_argmax(v):
        m = jnp.max(v, axis=-1, keepdims=True)
        return m, jnp.min(jnp.where(v == m, lane, far), axis=-1, keepdims=True)

    is_group = (lane >= N_EXPERTS) & (lane < N_EXPERTS + N_GROUPS)
    gl = jnp.where(is_group, lg, NEG_BIG)
    gmax, glane = first_argmax(gl)
    gidx = glane - N_EXPERTS
    g_prob = 1.0 / jnp.sum(jnp.where(is_group, jnp.exp(gl - gmax), 0.0), axis=-1, keepdims=True)

    in_group = (lane < N_EXPERTS) & ((lane // EXPERTS_PER_GROUP) == gidx)
    el = jnp.where(in_group, lg, NEG_BIG)
    m0, e0 = first_argmax(el)
    m1, e1 = first_argmax(jnp.where(lane == e0, NEG_BIG, el))
    ex = jnp.exp(m1 - m0)
    w0 = g_prob / (1.0 + ex)
    w1 = w0 * ex

    a0 = (lane == e0).astype(F32)
    a1 = (lane == e1).astype(F32)
    c0 = jnp.sum(a0, axis=0, keepdims=True)
    c1 = jnp.sum(a1, axis=0, keepdims=True)
    rows8 = jnp.floor((c0 + c1 + (SUBLANES - 1)) * (1.0 / SUBLANES)) * SUBLANES
    ei = lax.broadcasted_iota(I32, (LANES, LANES), 0)
    ej = lax.broadcasted_iota(I32, (LANES, LANES), 1)
    earlier_expert = (ei < ej).astype(BF16)
    start = _dot(jnp.broadcast_to(rows8, (SUBLANES, LANES)).astype(BF16), earlier_expert)[0:1, :]
    ti = lax.broadcasted_iota(I32, (tm, tm), 0)
    tj = lax.broadcasted_iota(I32, (tm, tm), 1)
    before = (tj < ti).astype(BF16)
    rank = _dot(before, jnp.concatenate([a0, a1], axis=1).astype(BF16))
    p0 = jnp.sum(a0 * (rank[:, :LANES] + start), axis=-1, keepdims=True)
    p1 = jnp.sum(a1 * (rank[:, LANES:] + start + c0), axis=-1, keepdims=True)

    meta = jnp.zeros((tm, LANES), F32)
    for pos, val in ((META_P0, p0), (META_P1, p1), (META_W0, w0), (META_W1, w1)):
        meta = jnp.where(lane == pos, val, meta)
    meta_ref[...] = meta
    sub = lax.broadcasted_iota(I32, (SUBLANES, LANES), 0)
    tab_ref[0] = jnp.where(sub == TAB_ROWS, rows8, jnp.where(sub == TAB_START, start, 0.0))

    meta_t = meta.T
    ri = lax.broadcasted_iota(I32, (LOCAL_ROWS, tm), 0).astype(F32)
    take = (ri == meta_t[META_P0:META_P0 + 1, :]) | (ri == meta_t[META_P1:META_P1 + 1, :])
    gathered = _dot(jnp.where(take, 1.0, 0.0).astype(BF16), u.astype(BF16))
    su_ref[...] = _pack_pairs(gathered)


def _merge_kernel(glu_ref, ys_ref, gc_ref, x_ref, dw_ref, db_ref, lng_ref, lnb_ref, wc_ref, wo_ref,
                  nf_ref, wrh_ref, wrl_ref, rb_ref, h_ref, su_ref, meta_ref, tab_ref, ext_scr, c_scr):
    tm = x_ref.shape[0]

    @pl.when(pl.program_id(1) == 0)
    def _():
        ext_scr[:, 0:CONF_HALO, :] = jnp.zeros((N_SLABS, CONF_HALO, LANES), F32)

    for j in range(N_SLABS):
        ext_scr[j, CONF_HALO:CONF_HALO + tm, :] = glu_ref[:, j * LANES:(j + 1) * LANES].astype(F32)

    rc = 128
    first = CONF_HALO - (CONF_KERNEL - 1)

    def slab(j, carry):
        for r in range(tm // rc):
            acc = jnp.zeros((rc, LANES), F32)
            for k in range(CONF_KERNEL):
                acc = acc + dw_ref[j, k:k + 1, :] * ext_scr[j, pl.ds(r * rc + first + k, rc), :]
            c_scr[j, r * rc:(r + 1) * rc, :] = acc
        return carry

    lax.fori_loop(0, N_SLABS, slab, 0)
    for j in range(N_SLABS):
        ext_scr[j, 0:CONF_HALO, :] = ext_scr[j, tm:tm + CONF_HALO, :]

    c = jnp.concatenate([c_scr[j] for j in range(N_SLABS)], axis=1) + db_ref[...]
    mu = jnp.mean(c, axis=-1, keepdims=True)
    xc = c - mu
    yln = xc * lax.rsqrt(jnp.mean(xc * xc, axis=-1, keepdims=True) + EPS) * lng_ref[...] + lnb_ref[...]
    y_conv = _dot(_silu(yln).astype(BF16), wc_ref[...])
    mix = gc_ref[...].astype(F32) * y_conv + ys_ref[...].astype(F32)
    h = x_ref[...] + _dot(mix.astype(BF16), wo_ref[...])
    h_ref[...] = h
    u = _rms(h, nf_ref[...])
    u_hi, u_lo = _split_bf16(u, 2)
    lg = _dot(u_hi, wrh_ref[...]) + _dot(u_hi, wrl_ref[...]) + _dot(u_lo, wrh_ref[...]) + rb_ref[...]
    _route_and_sort(u, lg, su_ref, meta_ref, tab_ref)


def _merge(glu, ys, gc, x2, dw3, dw_b, ln_g, ln_b, w_conf, w_o, nf_g, wr_hi, wr_lo, rb, batch, seq):
    tm = ROW_TILE
    nst = seq // tm
    t = batch * seq
    ntile = t // tm
    tile = lambda b, s: b * nst + s
    row = lambda n: pl.BlockSpec((tm, n), lambda b, s: (tile(b, s), 0))
    return pl.pallas_call(
        _merge_kernel,
        grid=(batch, nst),
        in_specs=[row(D_MODEL), row(D_MODEL), row(D_MODEL), row(D_MODEL),
                  _const_spec((N_SLABS, CONF_HALO, LANES)), _const_spec((1, D_MODEL)),
                  _const_spec((1, D_MODEL)), _const_spec((1, D_MODEL)),
                  _const_spec((D_MODEL, D_MODEL)), _const_spec((D_MODEL, D_MODEL)),
                  _const_spec((1, D_MODEL)), _const_spec((D_MODEL, LANES)), _const_spec((D_MODEL, LANES)),
                  _const_spec((1, LANES))],
        out_specs=[row(D_MODEL),
                   pl.BlockSpec((LOCAL_ROWS, HALF), lambda b, s: (tile(b, s), 0)),
                   row(LANES),
                   pl.BlockSpec((1, SUBLANES, LANES), lambda b, s: (tile(b, s), 0, 0))],
        out_shape=[jax.ShapeDtypeStruct((t, D_MODEL), F32),
                   jax.ShapeDtypeStruct((ntile * LOCAL_ROWS, HALF), I32),
                   jax.ShapeDtypeStruct((t, LANES), F32),
                   jax.ShapeDtypeStruct((ntile, SUBLANES, LANES), F32)],
        scratch_shapes=[pltpu.VMEM((N_SLABS, CONF_HALO + tm, LANES), F32),
                        pltpu.VMEM((N_SLABS, tm, LANES), F32)],
        compiler_params=_params(("arbitrary", "arbitrary")),
        name="merge",
    )(glu, ys, gc, x2, dw3, dw_b, ln_g, ln_b, w_conf, w_o, nf_g, wr_hi, wr_lo, rb)


BIG_PIECE = MOE_BLOCK_ROWS


def _piece_loops(n_pieces_of, copy_of):
    def run(action):
        def chunk(e, c):
            def one(k, cc):
                action(copy_of(e, k))
                return cc
            lax.fori_loop(0, n_pieces_of(e), one, 0)
            return c
        lax.fori_loop(0, N_EXPERTS, chunk, 0)
    return run


def _regroup_kernel(start, dst_off, npc, zdst, znpc, bdst, bnpc, su_ref, xb_ref, zero_scr, sem):
    i = pl.program_id(0)
    piece = SUBLANES

    def rows(o, n):
        return pl.ds(pl.multiple_of(o, n), n)

    @pl.when(i == 0)
    def _():
        zero_scr[...] = jnp.zeros_like(zero_scr)
        zero_pieces = _piece_loops(
            lambda e: znpc[e],
            lambda e, k: pltpu.make_async_copy(zero_scr.at[pl.ds(0, piece)],
                                               xb_ref.at[rows(zdst[e] + k * piece, piece)], sem))
        zero_pieces(lambda cp: cp.start())

        def big(k):
            return pltpu.make_async_copy(zero_scr, xb_ref.at[rows(bdst[0] + k * BIG_PIECE, BIG_PIECE)], sem)

        def big_start(k, c):
            big(k).start()
            return c

        def big_wait(k, c):
            big(k).wait()
            return c

        lax.fori_loop(0, bnpc[0], big_start, 0)
        zero_pieces(lambda cp: cp.wait())
        lax.fori_loop(0, bnpc[0], big_wait, 0)

    base = i * N_EXPERTS
    data_pieces = _piece_loops(
        lambda e: npc[base + e],
        lambda e, k: pltpu.make_async_copy(su_ref.at[rows(start[base + e] + k * piece, piece)],
                                           xb_ref.at[rows(dst_off[base + e] + k * piece, piece)], sem))
    data_pieces(lambda cp: cp.start())
    data_pieces(lambda cp: cp.wait())


def _regroup(start, dst_off, npc, zdst, znpc, bdst, bnpc, su, out_rows):
    ntile = su.shape[0] // LOCAL_ROWS
    return pl.pallas_call(
        _regroup_kernel,
        grid_spec=pltpu.PrefetchScalarGridSpec(
            num_scalar_prefetch=7, grid=(ntile,),
            in_specs=[pl.BlockSpec((LOCAL_ROWS, HALF), lambda i, *_: (i, 0))],
            out_specs=pl.BlockSpec(memory_space=pl.ANY),
            scratch_shapes=[pltpu.VMEM((BIG_PIECE, HALF), I32), pltpu.SemaphoreType.DMA(())]),
        out_shape=jax.ShapeDtypeStruct((out_rows, HALF), I32),
        compiler_params=_params(("arbitrary",)),
        name="regroup",
    )(start, dst_off, npc, zdst, znpc, bdst, bnpc, su)


def _ffn_kernel(be_ref, nv_ref, x_ref, wg_ref, wu_ref, wd_ref, y_ref):
    del be_ref

    @pl.when(pl.program_id(0) < nv_ref[0])
    def _():
        x_lo, x_hi = _unpack_pairs(x_ref[...])
        wg = wg_ref[0].astype(BF16)
        wu = wu_ref[0].astype(BF16)
        a = _dot(x_lo, wg[:HALF]) + _dot(x_hi, wg[HALF:])
        b = _dot(x_lo, wu[:HALF]) + _dot(x_hi, wu[HALF:])
        y_ref[...] = _pack_pairs(_dot((_silu(a) * b).astype(BF16), wd_ref[0].astype(BF16)))

    @pl.when(pl.program_id(0) >= nv_ref[0])
    def _():
        y_ref[...] = jnp.zeros_like(y_ref)


def _ffn(blk_e, n_valid, xb, wg, wu, wd):
    rows = xb.shape[0]
    bk = MOE_BLOCK_ROWS
    xmap = lambda i, be, nv: (jnp.minimum(i, nv[0] - 1), 0)
    ymap = lambda i, be, nv: (i, 0)
    wmap = lambda i, be, nv: (be[i], 0, 0)
    return pl.pallas_call(
        _ffn_kernel,
        grid_spec=pltpu.PrefetchScalarGridSpec(
            num_scalar_prefetch=2,
            grid=(rows // bk,),
            in_specs=[pl.BlockSpec((bk, HALF), xmap),
                      pl.BlockSpec((1, D_MODEL, D_EXPERT), wmap),
                      pl.BlockSpec((1, D_MODEL, D_EXPERT), wmap),
                      pl.BlockSpec((1, D_EXPERT, D_MODEL), wmap)],
            out_specs=pl.BlockSpec((bk, HALF), ymap)),
        out_shape=jax.ShapeDtypeStruct((rows, HALF), I32),
        compiler_params=_params(("arbitrary",)),
        name="expert_ffn",
    )(blk_e, n_valid, xb, wg, wu, wd)


def _combine_kernel(start, src_off, npc, used, meta_ref, h_ref, p_ref, yb_ref, gp_ref, wpg_ref, wpp_ref,
                    gf_ref, o_ref, yl_ref, sel_scr, ple_scr, sem):
    tm = h_ref.shape[0]
    i = pl.program_id(0)
    slot = i % 2
    piece = SUBLANES

    def rows(o):
        return pl.ds(pl.multiple_of(o, piece), piece)

    def gather(tile, buf):
        base = tile * N_EXPERTS
        return _piece_loops(
            lambda e: npc[base + e],
            lambda e, k: pltpu.make_async_copy(yb_ref.at[rows(src_off[base + e] + k * piece)],
                                               yl_ref.at[buf, rows(start[base + e] + k * piece)],
                                               sem.at[buf]))

    def fetch(tile, buf):
        gather(tile, buf)(lambda cp: cp.start())

        def zero_tail(k, c):
            yl_ref[buf, rows(used[tile] + k * piece), :] = jnp.zeros((piece, HALF), I32)
            return c

        lax.fori_loop(0, (LOCAL_ROWS - used[tile]) // piece, zero_tail, 0)

    @pl.when(i == 0)
    def _():
        fetch(i, slot)

    @pl.when(i + 1 < pl.num_programs(0))
    def _():
        fetch(i + 1, 1 - slot)

    m = meta_ref[...]
    ci = lax.broadcasted_iota(I32, (tm, LOCAL_ROWS), 1).astype(F32)
    sel_scr[...] = (jnp.where(ci == m[:, META_P0:META_P0 + 1], m[:, META_W0:META_W0 + 1], 0.0)
                    + jnp.where(ci == m[:, META_P1:META_P1 + 1], m[:, META_W1:META_W1 + 1], 0.0)
                    ).astype(BF16)
    ple_scr[...] = _dot(p_ref[...].astype(BF16), wpp_ref[...])
    gather(i, slot)(lambda cp: cp.wait())

    y_lo, y_hi = _unpack_pairs(yl_ref[slot])
    moe = jnp.concatenate([_dot(sel_scr[...], y_lo), _dot(sel_scr[...], y_hi)], axis=1)
    h = h_ref[...] + moe
    gate = _sigmoid(_dot(_rms(h, gp_ref[...]).astype(BF16), wpg_ref[...]))
    h = h + ple_scr[...] * gate
    o_ref[...] = _rms(h, gf_ref[...])


def _combine(start, src_off, npc, used, meta, h1, p2, yb, g_ple, w_pg, w_pp, g_fin):
    t = h1.shape[0]
    tm = ROW_TILE
    row = lambda n: pl.BlockSpec((tm, n), lambda i, *_: (i, 0))
    return pl.pallas_call(
        _combine_kernel,
        grid_spec=pltpu.PrefetchScalarGridSpec(
            num_scalar_prefetch=4, grid=(t // tm,),
            in_specs=[row(LANES), row(D_MODEL), row(PLE_DIM), pl.BlockSpec(memory_space=pl.ANY),
                      _const_spec((1, D_MODEL)), _const_spec((D_MODEL, D_MODEL)),
                      _const_spec((PLE_DIM, D_MODEL)), _const_spec((1, D_MODEL))],
            out_specs=row(D_MODEL),
            scratch_shapes=[pltpu.VMEM((2, LOCAL_ROWS, HALF), I32), pltpu.VMEM((tm, LOCAL_ROWS), BF16),
                            pltpu.VMEM((tm, D_MODEL), F32), pltpu.SemaphoreType.DMA((2,))]),
        out_shape=jax.ShapeDtypeStruct((t, D_MODEL), F32),
        compiler_params=_params(("arbitrary",)),
        name="combine",
    )(start, src_off, npc, used, meta, h1, p2, yb, g_ple, w_pg, w_pp, g_fin)


def _row(v, n=None):
    v = v.astype(F32).reshape(1, -1)
    if n is not None and v.shape[1] < n:
        v = jnp.pad(v, ((0, 0), (0, n - v.shape[1])))
    return v


def _layer(h2, p2, batch, seq, norm_mix_g, w_in, ssd_conv_w, ssd_conv_b, ssd_dt_bias, ssd_a_log, ssd_d,
           ssd_norm_g, w_ssd_out, conf_dw_w, conf_dw_b, conf_ln_g, conf_ln_b, w_conf_out, w_o,
           norm_ffn_g, router_group_w, router_group_b, router_expert_w, router_expert_b,
           expert_w_gate, expert_w_up, expert_w_down, norm_ple_g, w_ple_gate, w_ple_proj, final_g):
    t = batch * seq
    tm = ROW_TILE
    ntile = t // tm
    w_main = jnp.concatenate([w_in[:, :OFF_DT], w_in[:, OFF_GLU:]], axis=1).astype(BF16)
    w_dt = jnp.pad(w_in[:, OFF_DT:OFF_GLU], ((0, 0), (0, LANES - SSD_HEADS))).astype(BF16)
    conv_tab = jnp.concatenate([ssd_conv_w.astype(F32), ssd_conv_b.astype(F32)[None, :]], axis=0)
    conv_tab = jnp.pad(conv_tab, ((0, SUBLANES - conv_tab.shape[0]), (0, 0)))
    conv_tab = conv_tab.reshape(SUBLANES, XBC_SLABS, LANES).transpose(1, 0, 2)
    d_full = jnp.repeat(ssd_d.astype(F32), SSD_HEADDIM).reshape(1, D_MODEL)
    head_of_lane = jnp.arange(D_MODEL) // SSD_HEADDIM
    expand = (jnp.arange(LANES)[:, None] == head_of_lane[None, :]).astype(BF16)
    dw3 = jnp.pad(conf_dw_w.astype(F32), ((0, CONF_HALO - CONF_KERNEL), (0, 0)))
    dw3 = dw3.reshape(CONF_HALO, N_SLABS, LANES).transpose(1, 0, 2)
    wr = jnp.concatenate([router_expert_w, router_group_w], axis=1).astype(F32)
    wr = jnp.pad(wr, ((0, 0), (0, LANES - wr.shape[1])))
    wr_hi = wr.astype(BF16)
    wr_lo = (wr - wr_hi.astype(F32)).astype(BF16)
    rb = _row(jnp.concatenate([router_expert_b, router_group_b]), LANES)

    z, xbc, dt, glu, gs, gc = _inproj(h2, _row(norm_mix_g), w_main, w_dt, _row(ssd_dt_bias, LANES))
    ys = _ssd(z, xbc, dt, gs, conv_tab, _row(ssd_a_log, LANES), d_full, expand,
              _row(ssd_norm_g), w_ssd_out.astype(BF16), batch, seq)
    h1, su, meta, tab = _merge(glu, ys, gc, h2, dw3, _row(conf_dw_b), _row(conf_ln_g), _row(conf_ln_b),
                               w_conf_out.astype(BF16), w_o.astype(BF16), _row(norm_ffn_g), wr_hi, wr_lo, rb,
                               batch, seq)

    bk = MOE_BLOCK_ROWS
    n_blocks = (2 * t + ntile * N_EXPERTS * (SUBLANES - 1)) // bk + N_EXPERTS + 1
    total_rows = n_blocks * bk
    rows8 = tab[:, TAB_ROWS, :N_EXPERTS].astype(I32)
    start = tab[:, TAB_START, :N_EXPERTS].astype(I32)
    per_expert = jnp.sum(rows8, axis=0)
    region = (per_expert + bk - 1) // bk * bk
    region_end = jnp.cumsum(region)
    region_start = region_end - region
    n_valid = (region_end[-1] // bk).astype(I32)
    global_off = region_start[None, :] + jnp.cumsum(rows8, axis=0) - rows8
    npc = (rows8 // SUBLANES).reshape(-1)
    start = start.reshape(-1)
    global_off = global_off.reshape(-1).astype(I32)
    blk = jnp.arange(n_blocks, dtype=I32)
    blk_e = jnp.sum((blk[:, None] * bk >= region_end[None, :]).astype(I32), axis=1)
    last_e = jnp.sum((((n_valid - 1) * bk) >= region_end).astype(I32))
    blk_e = jnp.minimum(jnp.where(blk < n_valid, blk_e, last_e), N_EXPERTS - 1).astype(I32)
    one = lambda v: jnp.reshape(v, (1,)).astype(I32)

    xb = _regroup(start, global_off, npc,
                  (region_start + per_expert).astype(I32), ((region - per_expert) // SUBLANES).astype(I32),
                  one(region_end[-1]), one(n_blocks - n_valid), su, total_rows)
    yb = _ffn(blk_e, one(n_valid), xb, expert_w_gate, expert_w_up, expert_w_down)
    used = jnp.sum(rows8, axis=1).astype(I32)
    return _combine(start, global_off, npc, used, meta, h1, p2, yb, _row(norm_ple_g),
                    w_ple_gate.astype(BF16), w_ple_proj.astype(BF16), final_g)


def kernel(x, p, norm_mix_g, w_in, ssd_conv_w, ssd_conv_b, ssd_dt_bias, ssd_a_log, ssd_d, ssd_norm_g,
           w_ssd_out, conf_dw_w, conf_dw_b, conf_ln_g, conf_ln_b, w_conf_out, w_o, norm_ffn_g,
           router_group_w, router_group_b, router_expert_w, router_expert_b, expert_w_gate, expert_w_up,
           expert_w_down, norm_ple_g, w_ple_gate, w_ple_proj, final_norm_g):
    batch, seq, d = x.shape
    depth = p.shape[0]
    assert d == D_MODEL and depth == 1, "single-layer block with D_MODEL features"
    assert seq % ROW_TILE == 0 and ROW_TILE % SSD_CHUNK == 0
    h2 = x.reshape(batch * seq, d)
    out = _layer(h2, p[0].reshape(batch * seq, PLE_DIM), batch, seq, norm_mix_g[0], w_in[0], ssd_conv_w[0],
                 ssd_conv_b[0], ssd_dt_bias[0], ssd_a_log[0], ssd_d[0], ssd_norm_g[0], w_ssd_out[0],
                 conf_dw_w[0], conf_dw_b[0], conf_ln_g[0], conf_ln_b[0], w_conf_out[0], w_o[0],
                 norm_ffn_g[0], router_group_w[0], router_group_b[0], router_expert_w[0],
                 router_expert_b[0], expert_w_gate[0], expert_w_up[0], expert_w_down[0], norm_ple_g[0],
                 w_ple_gate[0], w_ple_proj[0], _row(final_norm_g))
    return out.reshape(batch, seq, d)
```

```python
import jax
import jax.numpy as jnp
from jax import lax
from jax.experimental import pallas as pl
from jax.experimental.pallas import tpu as pltpu

F32 = jnp.float32
BF16 = jnp.bfloat16
I32 = jnp.int32
EPS = 1e-6

D_MODEL = 1024
PLE_DIM = 256
SSD_HEADDIM = 64
SSD_HEADS = 16
SSD_GROUPS = 2
SSD_STATE = 128
SSD_CONV = 4
SSD_CHUNK = 128
SSD_BC = SSD_GROUPS * SSD_STATE
SSD_CONV_CH = D_MODEL + 2 * SSD_BC
CONF_KERNEL = 31
N_GROUPS = 4
EXPERTS_PER_GROUP = 8
N_EXPERTS = 32
D_EXPERT = 512

OFF_XBC = D_MODEL
OFF_DT = OFF_XBC + SSD_CONV_CH
OFF_GLU = OFF_DT + SSD_HEADS
OFF_GATE = OFF_GLU + 2 * D_MODEL

LANES = 128
SUBLANES = 8
VMEM_LIMIT_BYTES = 56 * 1024 * 1024

ROW_TILE = 512
MOE_BLOCK_ROWS = 512
NEG_BIG = -1e30
HEADS_PER_SLAB = LANES // SSD_HEADDIM
N_SLABS = D_MODEL // LANES
XBC_SLABS = SSD_CONV_CH // LANES
CONF_HALO = 32
HALF = D_MODEL // 2
LOCAL_ROWS = 2 * ROW_TILE + N_EXPERTS * SUBLANES
HIGH_MASK = -65536


def _sigmoid(v):
    return jax.nn.sigmoid(v)


def _silu(v):
    return v * jax.nn.sigmoid(v)


def _rms(v, g):
    return v * lax.rsqrt(jnp.mean(v * v, axis=-1, keepdims=True) + EPS) * g


def _dot(a, b):
    return jnp.dot(a, b, preferred_element_type=F32)


def _split_bf16(v, parts):
    out = []
    r = v
    for _ in range(parts):
        h = r.astype(BF16)
        out.append(h)
        r = r - h.astype(F32)
    return out


def _pack_pairs(v):
    r = v.astype(BF16).astype(F32)
    lo_bits = lax.bitcast_convert_type(r[:, :HALF], I32)
    lo = lax.shift_right_logical(lo_bits, jnp.full_like(lo_bits, 16))
    hi = lax.bitcast_convert_type(r[:, HALF:], I32) & HIGH_MASK
    return hi | lo


def _unpack_pairs(w):
    lo = lax.bitcast_convert_type(lax.shift_left(w, jnp.full_like(w, 16)), F32)
    hi = lax.bitcast_convert_type(w & HIGH_MASK, F32)
    return lo.astype(BF16), hi.astype(BF16)


def _const_spec(shape):
    nd = len(shape)
    return pl.BlockSpec(shape, lambda *_: (0,) * nd, pipeline_mode=pl.Buffered(1))


def _params(sem):
    return pltpu.CompilerParams(dimension_semantics=sem, vmem_limit_bytes=VMEM_LIMIT_BYTES)


def _inproj_kernel(x_ref, g_ref, w_ref, wdt_ref, dtb_ref,
                   z_ref, xbc_ref, dt_ref, glu_ref, gs_ref, gc_ref, u_scr):
    u_scr[...] = _rms(x_ref[...], g_ref[...]).astype(BF16)

    def seg(lo, n):
        return _dot(u_scr[...], w_ref[:, lo:lo + n])

    z_ref[...] = seg(0, D_MODEL).astype(BF16)
    blk = 512
    for j in range(SSD_CONV_CH // blk):
        xbc_ref[:, j * blk:(j + 1) * blk] = seg(D_MODEL + j * blk, blk).astype(BF16)
    o = D_MODEL + SSD_CONV_CH
    glu_ref[...] = (seg(o, D_MODEL) * _sigmoid(seg(o + D_MODEL, D_MODEL))).astype(BF16)
    gs_ref[...] = _sigmoid(seg(o + 2 * D_MODEL, D_MODEL)).astype(BF16)
    gc_ref[...] = _sigmoid(seg(o + 3 * D_MODEL, D_MODEL)).astype(BF16)
    dt_ref[...] = jax.nn.softplus(_dot(u_scr[...], wdt_ref[...]) + dtb_ref[...])


def _inproj(x2, g, w_main, w_dt, dt_bias):
    t = x2.shape[0]
    tm = ROW_TILE
    n_main = w_main.shape[1]
    row = lambda n: pl.BlockSpec((tm, n), lambda i: (i, 0))
    outs = [(D_MODEL, BF16), (SSD_CONV_CH, BF16), (LANES, F32), (D_MODEL, BF16), (D_MODEL, BF16), (D_MODEL, BF16)]
    return pl.pallas_call(
        _inproj_kernel,
        grid=(t // tm,),
        in_specs=[row(D_MODEL), _const_spec((1, D_MODEL)), _const_spec((D_MODEL, n_main)),
                  _const_spec((D_MODEL, LANES)), _const_spec((1, LANES))],
        out_specs=[row(n) for n, _ in outs],
        out_shape=[jax.ShapeDtypeStruct((t, n), d) for n, d in outs],
        scratch_shapes=[pltpu.VMEM((tm, D_MODEL), BF16)],
        compiler_params=_params(("parallel",)),
        name="inproj",
    )(x2, g, w_main, w_dt, dt_bias)


CONV_BIAS_ROW = SSD_CONV


def _ssd_kernel(z_ref, xbc_ref, dt_ref, gs_ref, cw_ref, alog_ref, dfull_ref, expand_ref,
                ng_ref, wout_ref, ys_ref, ext_scr, act_scr, state_scr, y_scr):
    tc = z_ref.shape[0]
    q = SSD_CHUNK

    @pl.when(pl.program_id(1) == 0)
    def _():
        ext_scr[:, 0:SUBLANES, :] = jnp.zeros((XBC_SLABS, SUBLANES, LANES), F32)
        state_scr[...] = jnp.zeros_like(state_scr)

    for j in range(XBC_SLABS):
        ext_scr[j, SUBLANES:SUBLANES + tc, :] = xbc_ref[:, j * LANES:(j + 1) * LANES].astype(F32)
    first = SUBLANES - (SSD_CONV - 1)

    def conv_slab(j, carry):
        for r in range(tc // q):
            acc = jnp.broadcast_to(cw_ref[j, CONV_BIAS_ROW:CONV_BIAS_ROW + 1, :], (q, LANES))
            for k in range(SSD_CONV):
                acc = acc + cw_ref[j, k:k + 1, :] * ext_scr[j, pl.ds(r * q + first + k, q), :]
            act_scr[j, r * q:(r + 1) * q, :] = _silu(acc)
        return carry

    lax.fori_loop(0, XBC_SLABS, conv_slab, 0)
    for j in range(XBC_SLABS):
        ext_scr[j, 0:SUBLANES, :] = ext_scr[j, tc:tc + SUBLANES, :]

    rowi = lax.broadcasted_iota(I32, (q, q), 0)
    coli = lax.broadcasted_iota(I32, (q, q), 1)
    causal = coli <= rowi
    tril = causal.astype(BF16)
    left = coli < SSD_HEADDIM
    a_neg = -jnp.exp(alog_ref[...])

    def chunk(c, carry):
        r0 = pl.multiple_of(c * q, q)
        rows = pl.ds(r0, q)
        dt = dt_ref[rows, :]
        parts = _split_bf16(dt * a_neg, 3)
        a_cs = _dot(tril, parts[0]) + _dot(tril, parts[1]) + _dot(tril, parts[2])
        a_cs_t = a_cs.T
        dt_t = dt.T
        w_t = jnp.exp(a_cs_t[:, q - 1:q] - a_cs_t) * dt_t
        cd8 = jnp.broadcast_to(jnp.exp(a_cs[q - 1:q, :]), (SUBLANES, LANES))
        cdp = _split_bf16(cd8, 3)
        cd_full = (_dot(cdp[0], expand_ref[...]) + _dot(cdp[1], expand_ref[...])
                   + _dot(cdp[2], expand_ref[...]))[0:1, :]

        slabs_per_group = N_SLABS // SSD_GROUPS
        for g in range(SSD_GROUPS):
            bg = act_scr[N_SLABS + g, rows, :]
            cg = act_scr[N_SLABS + SSD_GROUPS + g, rows, :]
            cb = lax.dot_general(cg.astype(BF16), bg.astype(BF16), (((1,), (1,)), ((), ())),
                                 preferred_element_type=F32)
            bg_t = bg.T
            for jj in range(slabs_per_group):
                j = g * slabs_per_group + jj
                ls = slice(j * LANES, (j + 1) * LANES)
                m_parts, c_parts, b_parts = [], [], []
                for hh in range(HEADS_PER_SLAB):
                    h = j * HEADS_PER_SLAB + hh
                    col = jnp.broadcast_to(a_cs[:, h:h + 1], (q, q))
                    seg = col - a_cs_t[h:h + 1, :]
                    dec = jnp.exp(jnp.where(causal, seg, NEG_BIG))
                    m_parts.append((cb * dec * dt_t[h:h + 1, :]).astype(BF16))
                    c_parts.append((cg * jnp.exp(col)).astype(BF16))
                    b_parts.append((bg_t * w_t[h:h + 1, :]).astype(BF16))
                xs = act_scr[j, rows, :]
                xs_b = xs.astype(BF16)
                st = state_scr[:, ls]
                st_b = st.astype(BF16)
                zero = jnp.zeros_like(xs_b)
                rhs_x = jnp.concatenate([jnp.where(left, xs_b, zero), jnp.where(left, zero, xs_b)], axis=0)
                rhs_s = jnp.concatenate([jnp.where(left, st_b, zero), jnp.where(left, zero, st_b)], axis=0)
                y = (_dot(jnp.concatenate(m_parts, axis=1), rhs_x)
                     + _dot(jnp.concatenate(c_parts, axis=1), rhs_s)
                     + dfull_ref[:, ls] * xs)
                y_scr[rows, ls] = y
                state_scr[:, ls] = st * cd_full[:, ls] + _dot(jnp.concatenate(b_parts, axis=1), rhs_x)
        return carry

    lax.fori_loop(0, tc // q, chunk, 0, unroll=True)

    v = y_scr[...] * _silu(z_ref[...].astype(F32))
    v = _rms(v, ng_ref[...])
    ys_ref[...] = (_dot(v.astype(BF16), wout_ref[...]) * gs_ref[...].astype(F32)).astype(BF16)


def _ssd(z, xbc, dt, gs, conv_tab, a_log, d_full, expand, norm_g, w_out, batch, seq):
    tc = ROW_TILE
    nst = seq // tc
    row = lambda n: pl.BlockSpec((tc, n), lambda b, s: (b * nst + s, 0))
    return pl.pallas_call(
        _ssd_kernel,
        grid=(batch, nst),
        in_specs=[row(D_MODEL), row(SSD_CONV_CH), row(LANES), row(D_MODEL),
                  _const_spec((XBC_SLABS, SUBLANES, LANES)),
                  _const_spec((1, LANES)), _const_spec((1, D_MODEL)), _const_spec((LANES, D_MODEL)),
                  _const_spec((1, D_MODEL)), _const_spec((D_MODEL, D_MODEL))],
        out_specs=row(D_MODEL),
        out_shape=jax.ShapeDtypeStruct((batch * seq, D_MODEL), BF16),
        scratch_shapes=[pltpu.VMEM((XBC_SLABS, SUBLANES + tc, LANES), F32),
                        pltpu.VMEM((XBC_SLABS, tc, LANES), F32),
                        pltpu.VMEM((SSD_STATE, D_MODEL), F32),
                        pltpu.VMEM((tc, D_MODEL), F32)],
        compiler_params=_params(("arbitrary", "arbitrary")),
        name="ssd",
    )(z, xbc, dt, gs, conv_tab, a_log, d_full, expand, norm_g, w_out)


META_P0, META_P1, META_W0, META_W1 = 0, 1, 2, 3
TAB_ROWS, TAB_START = 0, 1


def _route_and_sort(u, lg, su_ref, meta_ref, tab_ref):
    tm = u.shape[0]
    lane = lax.broadcasted_iota(I32, (tm, LANES), 1)
    far = jnp.int32(4 * LANES)

    def first_argmax(v):
        m = jnp.max(v, axis=-1, keepdims=True)
        return m, jnp.min(jnp.where(v == m, lane, far), axis=-1, keepdims=True)

    is_group = (lane >= N_EXPERTS) & (lane < N_EXPERTS + N_GROUPS)
    gl = jnp.where(is_group, lg, NEG_BIG)
    gmax, glane = first_argmax(gl)
    gidx = glane - N_EXPERTS
    g_prob = 1.0 / jnp.sum(jnp.where(is_group, jnp.exp(gl - gmax), 0.0), axis=-1, keepdims=True)

    in_group = (lane < N_EXPERTS) & ((lane // EXPERTS_PER_GROUP) == gidx)
    el = jnp.where(in_group, lg, NEG_BIG)
    m0, e0 = first_argmax(el)
    m1, e1 = first_argmax(jnp.where(lane == e0, NEG_BIG, el))
    ex = jnp.exp(m1 - m0)
    w0 = g_prob / (1.0 + ex)
    w1 = w0 * ex

    a0 = (lane == e0).astype(F32)
    a1 = (lane == e1).astype(F32)
    c0 = jnp.sum(a0, axis=0, keepdims=True)
    c1 = jnp.sum(a1, axis=0, keepdims=True)
    rows8 = jnp.floor((c0 + c1 + (SUBLANES - 1)) * (1.0 / SUBLANES)) * SUBLANES
    ei = lax.broadcasted_iota(I32, (LANES, LANES), 0)
    ej = lax.broadcasted_iota(I32, (LANES, LANES), 1)
    earlier_expert = (ei < ej).astype(BF16)
    start = _dot(jnp.broadcast_to(rows8, (SUBLANES, LANES)).astype(BF16), earlier_expert)[0:1, :]
    ti = lax.broadcasted_iota(I32, (tm, tm), 0)
    tj = lax.broadcasted_iota(I32, (tm, tm), 1)
    before = (tj < ti).astype(BF16)
    rank = _dot(before, jnp.concatenate([a0, a1], axis=1).astype(BF16))
    p0 = jnp.sum(a0 * (rank[:, :LANES] + start), axis=-1, keepdims=True)
    p1 = jnp.sum(a1 * (rank[:, LANES:] + start + c0), axis=-1, keepdims=True)

    meta = jnp.zeros((tm, LANES), F32)
    for pos, val in ((META_P0, p0), (META_P1, p1), (META_W0, w0), (META_W1, w1)):
        meta = jnp.where(lane == pos, val, meta)
    meta_ref[...] = meta
    sub = lax.broadcasted_iota(I32, (SUBLANES, LANES), 0)
    tab_ref[0] = jnp.where(sub == TAB_ROWS, rows8, jnp.where(sub == TAB_START, start, 0.0))

    meta_t = meta.T
    ri = lax.broadcasted_iota(I32, (LOCAL_ROWS, tm), 0).astype(F32)
    take = (ri == meta_t[META_P0:META_P0 + 1, :]) | (ri == meta_t[META_P1:META_P1 + 1, :])
    gathered = _dot(jnp.where(take, 1.0, 0.0).astype(BF16), u.astype(BF16))
    su_ref[...] = _pack_pairs(gathered)


def _merge_kernel(glu_ref, ys_ref, gc_ref, x_ref, dw_ref, db_ref, lng_ref, lnb_ref, wc_ref, wo_ref,
                  nf_ref, wrh_ref, wrl_ref, rb_ref, h_ref, su_ref, meta_ref, tab_ref, ext_scr, c_scr):
    tm = x_ref.shape[0]

    @pl.when(pl.program_id(1) == 0)
    def _():
        ext_scr[:, 0:CONF_HALO, :] = jnp.zeros((N_SLABS, CONF_HALO, LANES), F32)

    for j in range(N_SLABS):
        ext_scr[j, CONF_HALO:CONF_HALO + tm, :] = glu_ref[:, j * LANES:(j + 1) * LANES].astype(F32)

    rc = 128
    first = CONF_HALO - (CONF_KERNEL - 1)

    def slab(j, carry):
        for r in range(tm // rc):
            acc = jnp.zeros((rc, LANES), F32)
            for k in range(CONF_KERNEL):
                acc = acc + dw_ref[j, k:k + 1, :] * ext_scr[j, pl.ds(r * rc + first + k, rc), :]
            c_scr[j, r * rc:(r + 1) * rc, :] = acc
        return carry

    lax.fori_loop(0, N_SLABS, slab, 0)
    for j in range(N_SLABS):
        ext_scr[j, 0:CONF_HALO, :] = ext_scr[j, tm:tm + CONF_HALO, :]

    c = jnp.concatenate([c_scr[j] for j in range(N_SLABS)], axis=1) + db_ref[...]
    mu = jnp.mean(c, axis=-1, keepdims=True)
    xc = c - mu
    yln = xc * lax.rsqrt(jnp.mean(xc * xc, axis=-1, keepdims=True) + EPS) * lng_ref[...] + lnb_ref[...]
    y_conv = _dot(_silu(yln).astype(BF16), wc_ref[...])
    mix = gc_ref[...].astype(F32) * y_conv + ys_ref[...].astype(F32)
    h = x_ref[...] + _dot(mix.astype(BF16), wo_ref[...])
    h_ref[...] = h
    u = _rms(h, nf_ref[...])
    u_hi, u_lo = _split_bf16(u, 2)
    lg = _dot(u_hi, wrh_ref[...]) + _dot(u_hi, wrl_ref[...]) + _dot(u_lo, wrh_ref[...]) + rb_ref[...]
    _route_and_sort(u, lg, su_ref, meta_ref, tab_ref)


def _merge(glu, ys, gc, x2, dw3, dw_b, ln_g, ln_b, w_conf, w_o, nf_g, wr_hi, wr_lo, rb, batch, seq):
    tm = ROW_TILE
    nst = seq // tm
    t = batch * seq
    ntile = t // tm
    tile = lambda b, s: b * nst + s
    row = lambda n: pl.BlockSpec((tm, n), lambda b, s: (tile(b, s), 0))
    return pl.pallas_call(
        _merge_kernel,
        grid=(batch, nst),
        in_specs=[row(D_MODEL), row(D_MODEL), row(D_MODEL), row(D_MODEL),
                  _const_spec((N_SLABS, CONF_HALO, LANES)), _const_spec((1, D_MODEL)),
                  _const_spec((1, D_MODEL)), _const_spec((1, D_MODEL)),
                  _const_spec((D_MODEL, D_MODEL)), _const_spec((D_MODEL, D_MODEL)),
                  _const_spec((1, D_MODEL)), _const_spec((D_MODEL, LANES)), _const_spec((D_MODEL, LANES)),
                  _const_spec((1, LANES))],
        out_specs=[row(D_MODEL),
                   pl.BlockSpec((LOCAL_ROWS, HALF), lambda b, s: (tile(b, s), 0)),
                   row(LANES),
                   pl.BlockSpec((1, SUBLANES, LANES), lambda b, s: (tile(b, s), 0, 0))],
        out_shape=[jax.ShapeDtypeStruct((t, D_MODEL), F32),
                   jax.ShapeDtypeStruct((ntile * LOCAL_ROWS, HALF), I32),
                   jax.ShapeDtypeStruct((t, LANES), F32),
                   jax.ShapeDtypeStruct((ntile, SUBLANES, LANES), F32)],
        scratch_shapes=[pltpu.VMEM((N_SLABS, CONF_HALO + tm, LANES), F32),
                        pltpu.VMEM((N_SLABS, tm, LANES), F32)],
        compiler_params=_params(("arbitrary", "arbitrary")),
        name="merge",
    )(glu, ys, gc, x2, dw3, dw_b, ln_g, ln_b, w_conf, w_o, nf_g, wr_hi, wr_lo, rb)


BIG_PIECE = MOE_BLOCK_ROWS


def _piece_loops(n_pieces_of, copy_of):
    def run(action):
        def chunk(e, c):
            def one(k, cc):
                action(copy_of(e, k))
                return cc
            lax.fori_loop(0, n_pieces_of(e), one, 0)
            return c
        lax.fori_loop(0, N_EXPERTS, chunk, 0)
    return run


WAIT_SIZES = tuple(SUBLANES << b for b in range((LOCAL_ROWS // SUBLANES).bit_length() - 1, -1, -1))


def _wait_rows(n_rows, copy_of_rows):
    for size in WAIT_SIZES:
        @pl.when((n_rows & size) != 0)
        def _(size=size):
            copy_of_rows(size).wait()


def _regroup_kernel(start, dst_off, npc, used, zdst, znpc, bdst, bnpc, su_ref, xb_ref, zero_scr, sem):
    i = pl.program_id(0)
    piece = SUBLANES

    def rows(o, n):
        return pl.ds(pl.multiple_of(o, n), n)

    @pl.when(i == 0)
    def _():
        zero_scr[...] = jnp.zeros_like(zero_scr)
        zero_pieces = _piece_loops(
            lambda e: znpc[e],
            lambda e, k: pltpu.make_async_copy(zero_scr.at[pl.ds(0, piece)],
                                               xb_ref.at[rows(zdst[e] + k * piece, piece)], sem))
        zero_pieces(lambda cp: cp.start())

        def big(k):
            return pltpu.make_async_copy(zero_scr, xb_ref.at[rows(bdst[0] + k * BIG_PIECE, BIG_PIECE)], sem)

        def big_start(k, c):
            big(k).start()
            return c

        def big_wait(k, c):
            big(k).wait()
            return c

        lax.fori_loop(0, bnpc[0], big_start, 0)
        zero_pieces(lambda cp: cp.wait())
        lax.fori_loop(0, bnpc[0], big_wait, 0)

    base = i * N_EXPERTS
    data_pieces = _piece_loops(
        lambda e: npc[base + e],
        lambda e, k: pltpu.make_async_copy(su_ref.at[rows(start[base + e] + k * piece, piece)],
                                           xb_ref.at[rows(dst_off[base + e] + k * piece, piece)], sem))
    data_pieces(lambda cp: cp.start())
    _wait_rows(used[i], lambda n: pltpu.make_async_copy(su_ref.at[pl.ds(0, n)], xb_ref.at[pl.ds(0, n)], sem))


def _regroup(start, dst_off, npc, used, zdst, znpc, bdst, bnpc, su, out_rows):
    ntile = su.shape[0] // LOCAL_ROWS
    return pl.pallas_call(
        _regroup_kernel,
        grid_spec=pltpu.PrefetchScalarGridSpec(
            num_scalar_prefetch=8, grid=(ntile,),
            in_specs=[pl.BlockSpec((LOCAL_ROWS, HALF), lambda i, *_: (i, 0))],
            out_specs=pl.BlockSpec(memory_space=pl.ANY),
            scratch_shapes=[pltpu.VMEM((BIG_PIECE, HALF), I32), pltpu.SemaphoreType.DMA(())]),
        out_shape=jax.ShapeDtypeStruct((out_rows, HALF), I32),
        compiler_params=_params(("arbitrary",)),
        name="regroup",
    )(start, dst_off, npc, used, zdst, znpc, bdst, bnpc, su)


def _ffn_kernel(be_ref, nv_ref, x_ref, wg_ref, wu_ref, wd_ref, y_ref):
    del be_ref

    @pl.when(pl.program_id(0) < nv_ref[0])
    def _():
        x_lo, x_hi = _unpack_pairs(x_ref[...])
        wg = wg_ref[0].astype(BF16)
        wu = wu_ref[0].astype(BF16)
        a = _dot(x_lo, wg[:HALF]) + _dot(x_hi, wg[HALF:])
        b = _dot(x_lo, wu[:HALF]) + _dot(x_hi, wu[HALF:])
        y_ref[...] = _pack_pairs(_dot((_silu(a) * b).astype(BF16), wd_ref[0].astype(BF16)))

    @pl.when(pl.program_id(0) >= nv_ref[0])
    def _():
        y_ref[...] = jnp.zeros_like(y_ref)


def _ffn(blk_e, n_valid, xb, wg, wu, wd):
    rows = xb.shape[0]
    bk = MOE_BLOCK_ROWS
    xmap = lambda i, be, nv: (jnp.minimum(i, nv[0] - 1), 0)
    ymap = lambda i, be, nv: (i, 0)
    wmap = lambda i, be, nv: (be[i], 0, 0)
    return pl.pallas_call(
        _ffn_kernel,
        grid_spec=pltpu.PrefetchScalarGridSpec(
            num_scalar_prefetch=2,
            grid=(rows // bk,),
            in_specs=[pl.BlockSpec((bk, HALF), xmap),
                      pl.BlockSpec((1, D_MODEL, D_EXPERT), wmap),
                      pl.BlockSpec((1, D_MODEL, D_EXPERT), wmap),
                      pl.BlockSpec((1, D_EXPERT, D_MODEL), wmap)],
            out_specs=pl.BlockSpec((bk, HALF), ymap)),
        out_shape=jax.ShapeDtypeStruct((rows, HALF), I32),
        compiler_params=_params(("arbitrary",)),
        name="expert_ffn",
    )(blk_e, n_valid, xb, wg, wu, wd)


def _combine_kernel(start, src_off, npc, used, meta_ref, h_ref, p_ref, yb_ref, gp_ref, wpg_ref, wpp_ref,
                    gf_ref, o_ref, yl_ref, sel_scr, ple_scr, sem):
    tm = h_ref.shape[0]
    i = pl.program_id(0)
    piece = SUBLANES

    def rows(o):
        return pl.ds(pl.multiple_of(o, piece), piece)

    base = i * N_EXPERTS
    pieces = _piece_loops(
        lambda e: npc[base + e],
        lambda e, k: pltpu.make_async_copy(yb_ref.at[rows(src_off[base + e] + k * piece)],
                                           yl_ref.at[rows(start[base + e] + k * piece)], sem))
    pieces(lambda cp: cp.start())

    def zero_tail(k, c):
        yl_ref[rows(used[i] + k * piece), :] = jnp.zeros((piece, HALF), I32)
        return c

    lax.fori_loop(0, (LOCAL_ROWS - used[i]) // piece, zero_tail, 0)

    m = meta_ref[...]
    ci = lax.broadcasted_iota(I32, (tm, LOCAL_ROWS), 1).astype(F32)
    sel_scr[...] = (jnp.where(ci == m[:, META_P0:META_P0 + 1], m[:, META_W0:META_W0 + 1], 0.0)
                    + jnp.where(ci == m[:, META_P1:META_P1 + 1], m[:, META_W1:META_W1 + 1], 0.0)
                    ).astype(BF16)
    ple_scr[...] = _dot(p_ref[...].astype(BF16), wpp_ref[...])
    _wait_rows(used[i], lambda n: pltpu.make_async_copy(yb_ref.at[pl.ds(0, n)], yl_ref.at[pl.ds(0, n)], sem))

    y_lo, y_hi = _unpack_pairs(yl_ref[...])
    moe = jnp.concatenate([_dot(sel_scr[...], y_lo), _dot(sel_scr[...], y_hi)], axis=1)
    h = h_ref[...] + moe
    gate = _sigmoid(_dot(_rms(h, gp_ref[...]).astype(BF16), wpg_ref[...]))
    h = h + ple_scr[...] * gate
    o_ref[...] = _rms(h, gf_ref[...])


def _combine(start, src_off, npc, used, meta, h1, p2, yb, g_ple, w_pg, w_pp, g_fin):
    t = h1.shape[0]
    tm = ROW_TILE
    row = lambda n: pl.BlockSpec((tm, n), lambda i, *_: (i, 0))
    return pl.pallas_call(
        _combine_kernel,
        grid_spec=pltpu.PrefetchScalarGridSpec(
            num_scalar_prefetch=4, grid=(t // tm,),
            in_specs=[row(LANES), row(D_MODEL), row(PLE_DIM), pl.BlockSpec(memory_space=pl.ANY),
                      _const_spec((1, D_MODEL)), _const_spec((D_MODEL, D_MODEL)),
                      _const_spec((PLE_DIM, D_MODEL)), _const_spec((1, D_MODEL))],
            out_specs=row(D_MODEL),
            scratch_shapes=[pltpu.VMEM((LOCAL_ROWS, HALF), I32), pltpu.VMEM((tm, LOCAL_ROWS), BF16),
                            pltpu.VMEM((tm, D_MODEL), F32), pltpu.SemaphoreType.DMA(())]),
        out_shape=jax.ShapeDtypeStruct((t, D_MODEL), F32),
        compiler_params=_params(("arbitrary",)),
        name="combine",
    )(start, src_off, npc, used, meta, h1, p2, yb, g_ple, w_pg, w_pp, g_fin)


def _row(v, n=None):
    v = v.astype(F32).reshape(1, -1)
    if n is not None and v.shape[1] < n:
        v = jnp.pad(v, ((0, 0), (0, n - v.shape[1])))
    return v


def _layer(h2, p2, batch, seq, norm_mix_g, w_in, ssd_conv_w, ssd_conv_b, ssd_dt_bias, ssd_a_log, ssd_d,
           ssd_norm_g, w_ssd_out, conf_dw_w, conf_dw_b, conf_ln_g, conf_ln_b, w_conf_out, w_o,
           norm_ffn_g, router_group_w, router_group_b, router_expert_w, router_expert_b,
           expert_w_gate, expert_w_up, expert_w_down, norm_ple_g, w_ple_gate, w_ple_proj, final_g):
    t = batch * seq
    tm = ROW_TILE
    ntile = t // tm
    w_main = jnp.concatenate([w_in[:, :OFF_DT], w_in[:, OFF_GLU:]], axis=1).astype(BF16)
    w_dt = jnp.pad(w_in[:, OFF_DT:OFF_GLU], ((0, 0), (0, LANES - SSD_HEADS))).astype(BF16)
    conv_tab = jnp.concatenate([ssd_conv_w.astype(F32), ssd_conv_b.astype(F32)[None, :]], axis=0)
    conv_tab = jnp.pad(conv_tab, ((0, SUBLANES - conv_tab.shape[0]), (0, 0)))
    conv_tab = conv_tab.reshape(SUBLANES, XBC_SLABS, LANES).transpose(1, 0, 2)
    d_full = jnp.repeat(ssd_d.astype(F32), SSD_HEADDIM).reshape(1, D_MODEL)
    head_of_lane = jnp.arange(D_MODEL) // SSD_HEADDIM
    expand = (jnp.arange(LANES)[:, None] == head_of_lane[None, :]).astype(BF16)
    dw3 = jnp.pad(conf_dw_w.astype(F32), ((0, CONF_HALO - CONF_KERNEL), (0, 0)))
    dw3 = dw3.reshape(CONF_HALO, N_SLABS, LANES).transpose(1, 0, 2)
    wr = jnp.concatenate([router_expert_w, router_group_w], axis=1).astype(F32)
    wr = jnp.pad(wr, ((0, 0), (0, LANES - wr.shape[1])))
    wr_hi = wr.astype(BF16)
    wr_lo = (wr - wr_hi.astype(F32)).astype(BF16)
    rb = _row(jnp.concatenate([router_expert_b, router_group_b]), LANES)

    z, xbc, dt, glu, gs, gc = _inproj(h2, _row(norm_mix_g), w_main, w_dt, _row(ssd_dt_bias, LANES))
    ys = _ssd(z, xbc, dt, gs, conv_tab, _row(ssd_a_log, LANES), d_full, expand,
              _row(ssd_norm_g), w_ssd_out.astype(BF16), batch, seq)
    h1, su, meta, tab = _merge(glu, ys, gc, h2, dw3, _row(conf_dw_b), _row(conf_ln_g), _row(conf_ln_b),
                               w_conf_out.astype(BF16), w_o.astype(BF16), _row(norm_ffn_g), wr_hi, wr_lo, rb,
                               batch, seq)

    bk = MOE_BLOCK_ROWS
    n_blocks = (2 * t + ntile * N_EXPERTS * (SUBLANES - 1)) // bk + N_EXPERTS + 1
    total_rows = n_blocks * bk
    rows8 = tab[:, TAB_ROWS, :N_EXPERTS].astype(I32)
    start = tab[:, TAB_START, :N_EXPERTS].astype(I32)
    per_expert = jnp.sum(rows8, axis=0)
    region = (per_expert + bk - 1) // bk * bk
    region_end = jnp.cumsum(region)
    region_start = region_end - region
    n_valid = (region_end[-1] // bk).astype(I32)
    global_off = region_start[None, :] + jnp.cumsum(rows8, axis=0) - rows8
    npc = (rows8 // SUBLANES).reshape(-1)
    start = start.reshape(-1)
    global_off = global_off.reshape(-1).astype(I32)
    blk = jnp.arange(n_blocks, dtype=I32)
    blk_e = jnp.sum((blk[:, None] * bk >= region_end[None, :]).astype(I32), axis=1)
    last_e = jnp.sum((((n_valid - 1) * bk) >= region_end).astype(I32))
    blk_e = jnp.minimum(jnp.where(blk < n_valid, blk_e, last_e), N_EXPERTS - 1).astype(I32)
    one = lambda v: jnp.reshape(v, (1,)).astype(I32)

    used = jnp.sum(rows8, axis=1).astype(I32)
    xb = _regroup(start, global_off, npc, used,
                  (region_start + per_expert).astype(I32), ((region - per_expert) // SUBLANES).astype(I32),
                  one(region_end[-1]), one(n_blocks - n_valid), su, total_rows)
    yb = _ffn(blk_e, one(n_valid), xb, expert_w_gate, expert_w_up, expert_w_down)
    return _combine(start, global_off, npc, used, meta, h1, p2, yb, _row(norm_ple_g),
                    w_ple_gate.astype(BF16), w_ple_proj.astype(BF16), final_g)


def kernel(x, p, norm_mix_g, w_in, ssd_conv_w, ssd_conv_b, ssd_dt_bias, ssd_a_log, ssd_d, ssd_norm_g,
           w_ssd_out, conf_dw_w, conf_dw_b, conf_ln_g, conf_ln_b, w_conf_out, w_o, norm_ffn_g,
           router_group_w, router_group_b, router_expert_w, router_expert_b, expert_w_gate, expert_w_up,
           expert_w_down, norm_ple_g, w_ple_gate, w_ple_proj, final_norm_g):
    batch, seq, d = x.shape
    depth = p.shape[0]
    assert d == D_MODEL and depth == 1, "single-layer block with D_MODEL features"
    assert seq % ROW_TILE == 0 and ROW_TILE % SSD_CHUNK == 0
    h2 = x.reshape(batch * seq, d)
    out = _layer(h2, p[0].reshape(batch * seq, PLE_DIM), batch, seq, norm_mix_g[0], w_in[0], ssd_conv_w[0],
                 ssd_conv_b[0], ssd_dt_bias[0], ssd_a_log[0], ssd_d[0], ssd_norm_g[0], w_ssd_out[0],
                 conf_dw_w[0], conf_dw_b[0], conf_ln_g[0], conf_ln_b[0], w_conf_out[0], w_o[0],
                 norm_ffn_g[0], router_group_w[0], router_group_b[0], router_expert_w[0],
                 router_expert_b[0], expert_w_gate[0], expert_w_up[0], expert_w_down[0], norm_ple_g[0],
                 w_ple_gate[0], w_ple_proj[0], _row(final_norm_g))
    return out.reshape(batch, seq, d)
```

```python
import jax
import jax.numpy as jnp
from jax import lax
from jax.experimental import pallas as pl
from jax.experimental.pallas import tpu as pltpu

F32 = jnp.float32
BF16 = jnp.bfloat16
I32 = jnp.int32
EPS = 1e-6

D_MODEL = 1024
PLE_DIM = 256
SSD_HEADDIM = 64
SSD_HEADS = 16
SSD_GROUPS = 2
SSD_STATE = 128
SSD_CONV = 4
SSD_CHUNK = 128
SSD_BC = SSD_GROUPS * SSD_STATE
SSD_CONV_CH = D_MODEL + 2 * SSD_BC
CONF_KERNEL = 31
N_GROUPS = 4
EXPERTS_PER_GROUP = 8
N_EXPERTS = 32
D_EXPERT = 512

OFF_XBC = D_MODEL
OFF_DT = OFF_XBC + SSD_CONV_CH
OFF_GLU = OFF_DT + SSD_HEADS
OFF_GATE = OFF_GLU + 2 * D_MODEL

LANES = 128
SUBLANES = 8
VMEM_LIMIT_BYTES = 56 * 1024 * 1024

ROW_TILE = 512
MOE_BLOCK_ROWS = 512
NEG_BIG = -1e30
HEADS_PER_SLAB = LANES // SSD_HEADDIM
N_SLABS = D_MODEL // LANES
XBC_SLABS = SSD_CONV_CH // LANES
CONF_HALO = 32
HALF = D_MODEL // 2
LOCAL_ROWS = 2 * ROW_TILE + N_EXPERTS * SUBLANES
HIGH_MASK = -65536


def _sigmoid(v):
    return jax.nn.sigmoid(v)


def _silu(v):
    return v * jax.nn.sigmoid(v)


def _rms(v, g):
    return v * lax.rsqrt(jnp.mean(v * v, axis=-1, keepdims=True) + EPS) * g


def _dot(a, b):
    return jnp.dot(a, b, preferred_element_type=F32)


def _split_bf16(v, parts):
    out = []
    r = v
    for _ in range(parts):
        h = r.astype(BF16)
        out.append(h)
        r = r - h.astype(F32)
    return out


def _pack_pairs(v, exact_bf16=False):
    r = v if exact_bf16 else v.astype(BF16).astype(F32)
    lo_bits = lax.bitcast_convert_type(r[:, :HALF], I32)
    lo = lax.shift_right_logical(lo_bits, jnp.full_like(lo_bits, 16))
    hi = lax.bitcast_convert_type(r[:, HALF:], I32) & HIGH_MASK
    return hi | lo


def _unpack_pairs(w):
    lo = lax.bitcast_convert_type(lax.shift_left(w, jnp.full_like(w, 16)), F32)
    hi = lax.bitcast_convert_type(w & HIGH_MASK, F32)
    return lo.astype(BF16), hi.astype(BF16)


def _const_spec(shape):
    nd = len(shape)
    return pl.BlockSpec(shape, lambda *_: (0,) * nd, pipeline_mode=pl.Buffered(1))


def _params(sem):
    return pltpu.CompilerParams(dimension_semantics=sem, vmem_limit_bytes=VMEM_LIMIT_BYTES)


def _inproj_kernel(x_ref, g_ref, w_ref, wdt_ref, dtb_ref,
                   z_ref, xbc_ref, dt_ref, glu_ref, gs_ref, gc_ref, u_scr):
    u_scr[...] = _rms(x_ref[...], g_ref[...]).astype(BF16)

    def seg(lo, n):
        return _dot(u_scr[...], w_ref[:, lo:lo + n])

    z_ref[...] = seg(0, D_MODEL).astype(BF16)
    blk = 512
    for j in range(SSD_CONV_CH // blk):
        xbc_ref[:, j * blk:(j + 1) * blk] = seg(D_MODEL + j * blk, blk).astype(BF16)
    o = D_MODEL + SSD_CONV_CH
    glu_ref[...] = (seg(o, D_MODEL) * _sigmoid(seg(o + D_MODEL, D_MODEL))).astype(BF16)
    gs_ref[...] = _sigmoid(seg(o + 2 * D_MODEL, D_MODEL)).astype(BF16)
    gc_ref[...] = _sigmoid(seg(o + 3 * D_MODEL, D_MODEL)).astype(BF16)
    dt_ref[...] = jax.nn.softplus(_dot(u_scr[...], wdt_ref[...]) + dtb_ref[...])


def _inproj(x2, g, w_main, w_dt, dt_bias):
    t = x2.shape[0]
    tm = ROW_TILE
    n_main = w_main.shape[1]
    row = lambda n: pl.BlockSpec((tm, n), lambda i: (i, 0))
    outs = [(D_MODEL, BF16), (SSD_CONV_CH, BF16), (LANES, F32), (D_MODEL, BF16), (D_MODEL, BF16), (D_MODEL, BF16)]
    return pl.pallas_call(
        _inproj_kernel,
        grid=(t // tm,),
        in_specs=[row(D_MODEL), _const_spec((1, D_MODEL)), _const_spec((D_MODEL, n_main)),
                  _const_spec((D_MODEL, LANES)), _const_spec((1, LANES))],
        out_specs=[row(n) for n, _ in outs],
        out_shape=[jax.ShapeDtypeStruct((t, n), d) for n, d in outs],
        scratch_shapes=[pltpu.VMEM((tm, D_MODEL), BF16)],
        compiler_params=_params(("parallel",)),
        name="inproj",
    )(x2, g, w_main, w_dt, dt_bias)


CONV_BIAS_ROW = SSD_CONV


def _ssd_kernel(z_ref, xbc_ref, dt_ref, gs_ref, cw_ref, alog_ref, dfull_ref, expand_ref,
                ng_ref, wout_ref, ys_ref, ext_scr, act_scr, state_scr, y_scr):
    tc = z_ref.shape[0]
    q = SSD_CHUNK

    @pl.when(pl.program_id(1) == 0)
    def _():
        ext_scr[:, 0:SUBLANES, :] = jnp.zeros((XBC_SLABS, SUBLANES, LANES), F32)
        state_scr[...] = jnp.zeros_like(state_scr)

    for j in range(XBC_SLABS):
        ext_scr[j, SUBLANES:SUBLANES + tc, :] = xbc_ref[:, j * LANES:(j + 1) * LANES].astype(F32)
    first = SUBLANES - (SSD_CONV - 1)

    def conv_slab(j, carry):
        for r in range(tc // q):
            acc = jnp.broadcast_to(cw_ref[j, CONV_BIAS_ROW:CONV_BIAS_ROW + 1, :], (q, LANES))
            for k in range(SSD_CONV):
                acc = acc + cw_ref[j, k:k + 1, :] * ext_scr[j, pl.ds(r * q + first + k, q), :]
            act_scr[j, r * q:(r + 1) * q, :] = _silu(acc)
        return carry

    lax.fori_loop(0, XBC_SLABS, conv_slab, 0)
    for j in range(XBC_SLABS):
        ext_scr[j, 0:SUBLANES, :] = ext_scr[j, tc:tc + SUBLANES, :]

    rowi = lax.broadcasted_iota(I32, (q, q), 0)
    coli = lax.broadcasted_iota(I32, (q, q), 1)
    causal = coli <= rowi
    tril = causal.astype(BF16)
    left = coli < SSD_HEADDIM
    a_neg = -jnp.exp(alog_ref[...])

    def chunk(c, carry):
        r0 = pl.multiple_of(c * q, q)
        rows = pl.ds(r0, q)
        dt = dt_ref[rows, :]
        parts = _split_bf16(dt * a_neg, 3)
        a_cs = _dot(tril, parts[0]) + _dot(tril, parts[1]) + _dot(tril, parts[2])
        a_cs_t = a_cs.T
        dt_t = dt.T
        w_t = jnp.exp(a_cs_t[:, q - 1:q] - a_cs_t) * dt_t
        cd8 = jnp.broadcast_to(jnp.exp(a_cs[q - 1:q, :]), (SUBLANES, LANES))
        cdp = _split_bf16(cd8, 3)
        cd_full = (_dot(cdp[0], expand_ref[...]) + _dot(cdp[1], expand_ref[...])
                   + _dot(cdp[2], expand_ref[...]))[0:1, :]

        slabs_per_group = N_SLABS // SSD_GROUPS
        for g in range(SSD_GROUPS):
            bg = act_scr[N_SLABS + g, rows, :]
            cg = act_scr[N_SLABS + SSD_GROUPS + g, rows, :]
            cb = lax.dot_general(cg.astype(BF16), bg.astype(BF16), (((1,), (1,)), ((), ())),
                                 preferred_element_type=F32)
            bg_t = bg.T
            for jj in range(slabs_per_group):
                j = g * slabs_per_group + jj
                ls = slice(j * LANES, (j + 1) * LANES)
                m_parts, c_parts, b_parts = [], [], []
                for hh in range(HEADS_PER_SLAB):
                    h = j * HEADS_PER_SLAB + hh
                    col = jnp.broadcast_to(a_cs[:, h:h + 1], (q, q))
                    seg = col - a_cs_t[h:h + 1, :]
                    dec = jnp.exp(jnp.where(causal, seg, NEG_BIG))
                    m_parts.append((cb * dec * dt_t[h:h + 1, :]).astype(BF16))
                    c_parts.append((cg * jnp.exp(col)).astype(BF16))
                    b_parts.append((bg_t * w_t[h:h + 1, :]).astype(BF16))
                xs = act_scr[j, rows, :]
                xs_b = xs.astype(BF16)
                st = state_scr[:, ls]
                st_b = st.astype(BF16)
                zero = jnp.zeros_like(xs_b)
                rhs_x = jnp.concatenate([jnp.where(left, xs_b, zero), jnp.where(left, zero, xs_b)], axis=0)
                rhs_s = jnp.concatenate([jnp.where(left, st_b, zero), jnp.where(left, zero, st_b)], axis=0)
                y = (_dot(jnp.concatenate(m_parts, axis=1), rhs_x)
                     + _dot(jnp.concatenate(c_parts, axis=1), rhs_s)
                     + dfull_ref[:, ls] * xs)
                y_scr[rows, ls] = y
                state_scr[:, ls] = st * cd_full[:, ls] + _dot(jnp.concatenate(b_parts, axis=1), rhs_x)
        return carry

    lax.fori_loop(0, tc // q, chunk, 0, unroll=True)

    v = y_scr[...] * _silu(z_ref[...].astype(F32))
    v = _rms(v, ng_ref[...])
    ys_ref[...] = (_dot(v.astype(BF16), wout_ref[...]) * gs_ref[...].astype(F32)).astype(BF16)


def _ssd(z, xbc, dt, gs, conv_tab, a_log, d_full, expand, norm_g, w_out, batch, seq):
    tc = ROW_TILE
    nst = seq // tc
    row = lambda n: pl.BlockSpec((tc, n), lambda b, s: (b * nst + s, 0))
    return pl.pallas_call(
        _ssd_kernel,
        grid=(batch, nst),
        in_specs=[row(D_MODEL), row(SSD_CONV_CH), row(LANES), row(D_MODEL),
                  _const_spec((XBC_SLABS, SUBLANES, LANES)),
                  _const_spec((1, LANES)), _const_spec((1, D_MODEL)), _const_spec((LANES, D_MODEL)),
                  _const_spec((1, D_MODEL)), _const_spec((D_MODEL, D_MODEL))],
        out_specs=row(D_MODEL),
        out_shape=jax.ShapeDtypeStruct((batch * seq, D_MODEL), BF16),
        scratch_shapes=[pltpu.VMEM((XBC_SLABS, SUBLANES + tc, LANES), F32),
                        pltpu.VMEM((XBC_SLABS, tc, LANES), F32),
                        pltpu.VMEM((SSD_STATE, D_MODEL), F32),
                        pltpu.VMEM((tc, D_MODEL), F32)],
        compiler_params=_params(("arbitrary", "arbitrary")),
        name="ssd",
    )(z, xbc, dt, gs, conv_tab, a_log, d_full, expand, norm_g, w_out)


META_P0, META_P1, META_W0, META_W1 = 0, 1, 2, 3
TAB_ROWS, TAB_START = 0, 1


def _route_and_sort(u, lg, su_ref, meta_ref, tab_ref):
    tm = u.shape[0]
    lane = lax.broadcasted_iota(I32, (tm, LANES), 1)
    far = jnp.int32(4 * LANES)

    def first_argmax(v):
        m = jnp.max(v, axis=-1, keepdims=True)
        return m, jnp.min(jnp.where(v == m, lane, far), axis=-1, keepdims=True)

    is_group = (lane >= N_EXPERTS) & (lane < N_EXPERTS + N_GROUPS)
    gl = jnp.where(is_group, lg, NEG_BIG)
    gmax, glane = first_argmax(gl)
    gidx = glane - N_EXPERTS
    g_prob = 1.0 / jnp.sum(jnp.where(is_group, jnp.exp(gl - gmax), 0.0), axis=-1, keepdims=True)

    in_group = (lane < N_EXPERTS) & ((lane // EXPERTS_PER_GROUP) == gidx)
    el = jnp.where(in_group, lg, NEG_BIG)
    m0, e0 = first_argmax(el)
    m1, e1 = first_argmax(jnp.where(lane == e0, NEG_BIG, el))
    ex = jnp.exp(m1 - m0)
    w0 = g_prob / (1.0 + ex)
    w1 = w0 * ex

    a0 = (lane == e0).astype(F32)
    a1 = (lane == e1).astype(F32)
    c0 = jnp.sum(a0, axis=0, keepdims=True)
    c1 = jnp.sum(a1, axis=0, keepdims=True)
    rows8 = jnp.floor((c0 + c1 + (SUBLANES - 1)) * (1.0 / SUBLANES)) * SUBLANES
    ei = lax.broadcasted_iota(I32, (LANES, LANES), 0)
    ej = lax.broadcasted_iota(I32, (LANES, LANES), 1)
    earlier_expert = (ei < ej).astype(BF16)
    start = _dot(jnp.broadcast_to(rows8, (SUBLANES, LANES)).astype(BF16), earlier_expert)[0:1, :]
    ti = lax.broadcasted_iota(I32, (tm, tm), 0)
    tj = lax.broadcasted_iota(I32, (tm, tm), 1)
    before = (tj < ti).astype(BF16)
    rank = _dot(before, jnp.concatenate([a0, a1], axis=1).astype(BF16))
    p0 = jnp.sum(a0 * (rank[:, :LANES] + start), axis=-1, keepdims=True)
    p1 = jnp.sum(a1 * (rank[:, LANES:] + start + c0), axis=-1, keepdims=True)

    meta = jnp.zeros((tm, LANES), F32)
    for pos, val in ((META_P0, p0), (META_P1, p1), (META_W0, w0), (META_W1, w1)):
        meta = jnp.where(lane == pos, val, meta)
    meta_ref[...] = meta
    sub = lax.broadcasted_iota(I32, (SUBLANES, LANES), 0)
    tab_ref[0] = jnp.where(sub == TAB_ROWS, rows8, jnp.where(sub == TAB_START, start, 0.0))

    meta_t = meta.T
    ri = lax.broadcasted_iota(I32, (LOCAL_ROWS, tm), 0).astype(F32)
    take = (ri == meta_t[META_P0:META_P0 + 1, :]) | (ri == meta_t[META_P1:META_P1 + 1, :])
    gathered = _dot(jnp.where(take, 1.0, 0.0).astype(BF16), u.astype(BF16))
    su_ref[...] = _pack_pairs(gathered, exact_bf16=True)


def _merge_kernel(glu_ref, ys_ref, gc_ref, x_ref, dw_ref, db_ref, lng_ref, lnb_ref, wc_ref, wo_ref,
                  nf_ref, wr_ref, rb_ref, h_ref, su_ref, meta_ref, tab_ref, ext_scr, c_scr):
    tm = x_ref.shape[0]

    @pl.when(pl.program_id(1) == 0)
    def _():
        ext_scr[:, 0:CONF_HALO, :] = jnp.zeros((N_SLABS, CONF_HALO, LANES), F32)

    for j in range(N_SLABS):
        ext_scr[j, CONF_HALO:CONF_HALO + tm, :] = glu_ref[:, j * LANES:(j + 1) * LANES].astype(F32)

    rc = 128
    first = CONF_HALO - (CONF_KERNEL - 1)

    def slab(j, carry):
        for r in range(tm // rc):
            acc = jnp.zeros((rc, LANES), F32)
            for k in range(CONF_KERNEL):
                acc = acc + dw_ref[j, k:k + 1, :] * ext_scr[j, pl.ds(r * rc + first + k, rc), :]
            c_scr[j, r * rc:(r + 1) * rc, :] = acc
        return carry

    lax.fori_loop(0, N_SLABS, slab, 0)
    for j in range(N_SLABS):
        ext_scr[j, 0:CONF_HALO, :] = ext_scr[j, tm:tm + CONF_HALO, :]

    c = jnp.concatenate([c_scr[j] for j in range(N_SLABS)], axis=1) + db_ref[...]
    mu = jnp.mean(c, axis=-1, keepdims=True)
    xc = c - mu
    yln = xc * lax.rsqrt(jnp.mean(xc * xc, axis=-1, keepdims=True) + EPS) * lng_ref[...] + lnb_ref[...]
    y_conv = _dot(_silu(yln).astype(BF16), wc_ref[...])
    mix = gc_ref[...].astype(F32) * y_conv + ys_ref[...].astype(F32)
    h = x_ref[...] + _dot(mix.astype(BF16), wo_ref[...])
    h_ref[...] = h
    u = _rms(h, nf_ref[...])
    u_hi, u_lo = _split_bf16(u, 2)
    both = _dot(u_hi, wr_ref[...])
    lg = both[:, :LANES] + both[:, LANES:] + _dot(u_lo, wr_ref[:, :LANES]) + rb_ref[...]
    _route_and_sort(u, lg, su_ref, meta_ref, tab_ref)


def _merge(glu, ys, gc, x2, dw3, dw_b, ln_g, ln_b, w_conf, w_o, nf_g, wr_cat, rb, batch, seq):
    tm = ROW_TILE
    nst = seq // tm
    t = batch * seq
    ntile = t // tm
    tile = lambda b, s: b * nst + s
    row = lambda n: pl.BlockSpec((tm, n), lambda b, s: (tile(b, s), 0))
    return pl.pallas_call(
        _merge_kernel,
        grid=(batch, nst),
        in_specs=[row(D_MODEL), row(D_MODEL), row(D_MODEL), row(D_MODEL),
                  _const_spec((N_SLABS, CONF_HALO, LANES)), _const_spec((1, D_MODEL)),
                  _const_spec((1, D_MODEL)), _const_spec((1, D_MODEL)),
                  _const_spec((D_MODEL, D_MODEL)), _const_spec((D_MODEL, D_MODEL)),
                  _const_spec((1, D_MODEL)), _const_spec((D_MODEL, 2 * LANES)), _const_spec((1, LANES))],
        out_specs=[row(D_MODEL),
                   pl.BlockSpec((LOCAL_ROWS, HALF), lambda b, s: (tile(b, s), 0)),
                   row(LANES),
                   pl.BlockSpec((1, SUBLANES, LANES), lambda b, s: (tile(b, s), 0, 0))],
        out_shape=[jax.ShapeDtypeStruct((t, D_MODEL), F32),
                   jax.ShapeDtypeStruct((ntile * LOCAL_ROWS, HALF), I32),
                   jax.ShapeDtypeStruct((t, LANES), F32),
                   jax.ShapeDtypeStruct((ntile, SUBLANES, LANES), F32)],
        scratch_shapes=[pltpu.VMEM((N_SLABS, CONF_HALO + tm, LANES), F32),
                        pltpu.VMEM((N_SLABS, tm, LANES), F32)],
        compiler_params=_params(("arbitrary", "arbitrary")),
        name="merge",
    )(glu, ys, gc, x2, dw3, dw_b, ln_g, ln_b, w_conf, w_o, nf_g, wr_cat, rb)


BIG_PIECE = MOE_BLOCK_ROWS


LONG_COPY_ROWS = 4 * SUBLANES


def _piece_loops(n_pieces_of, copy_of):
    per_long = LONG_COPY_ROWS // SUBLANES

    def run(action):
        def chunk(e, c):
            n = n_pieces_of(e)
            n_long = lax.shift_right_logical(n, per_long.bit_length() - 1)

            def long_copy(k, cc):
                action(copy_of(e, k * LONG_COPY_ROWS, LONG_COPY_ROWS))
                return cc

            def short_copy(k, cc):
                action(copy_of(e, n_long * LONG_COPY_ROWS + k * SUBLANES, SUBLANES))
                return cc

            lax.fori_loop(0, n_long, long_copy, 0)
            lax.fori_loop(0, n & (per_long - 1), short_copy, 0)
            return c
        lax.fori_loop(0, N_EXPERTS, chunk, 0)
    return run


WAIT_SIZES = tuple(SUBLANES << b for b in range((LOCAL_ROWS // SUBLANES).bit_length() - 1, -1, -1))


def _wait_rows(n_rows, copy_of_rows):
    for size in WAIT_SIZES:
        @pl.when((n_rows & size) != 0)
        def _(size=size):
            copy_of_rows(size).wait()


def _regroup_kernel(start, dst_off, npc, used, zdst, znpc, bdst, bnpc, su_ref, xb_ref, zero_scr, sem):
    i = pl.program_id(0)

    def rows(o, n):
        return pl.ds(pl.multiple_of(o, SUBLANES), n)

    @pl.when(i == 0)
    def _():
        zero_scr[...] = jnp.zeros_like(zero_scr)
        zero_pieces = _piece_loops(
            lambda e: znpc[e],
            lambda e, off, n: pltpu.make_async_copy(zero_scr.at[pl.ds(0, n)],
                                                    xb_ref.at[rows(zdst[e] + off, n)], sem))
        zero_pieces(lambda cp: cp.start())

        def big(k):
            return pltpu.make_async_copy(zero_scr, xb_ref.at[rows(bdst[0] + k * BIG_PIECE, BIG_PIECE)], sem)

        def big_start(k, c):
            big(k).start()
            return c

        def big_wait(k, c):
            big(k).wait()
            return c

        lax.fori_loop(0, bnpc[0], big_start, 0)
        zero_pieces(lambda cp: cp.wait())
        lax.fori_loop(0, bnpc[0], big_wait, 0)

    base = i * N_EXPERTS
    data_pieces = _piece_loops(
        lambda e: npc[base + e],
        lambda e, off, n: pltpu.make_async_copy(su_ref.at[rows(start[base + e] + off, n)],
                                                xb_ref.at[rows(dst_off[base + e] + off, n)], sem))
    data_pieces(lambda cp: cp.start())
    _wait_rows(used[i], lambda n: pltpu.make_async_copy(su_ref.at[pl.ds(0, n)], xb_ref.at[pl.ds(0, n)], sem))


def _regroup(start, dst_off, npc, used, zdst, znpc, bdst, bnpc, su, out_rows):
    ntile = su.shape[0] // LOCAL_ROWS
    return pl.pallas_call(
        _regroup_kernel,
        grid_spec=pltpu.PrefetchScalarGridSpec(
            num_scalar_prefetch=8, grid=(ntile,),
            in_specs=[pl.BlockSpec((LOCAL_ROWS, HALF), lambda i, *_: (i, 0))],
            out_specs=pl.BlockSpec(memory_space=pl.ANY),
            scratch_shapes=[pltpu.VMEM((BIG_PIECE, HALF), I32), pltpu.SemaphoreType.DMA(())]),
        out_shape=jax.ShapeDtypeStruct((out_rows, HALF), I32),
        compiler_params=_params(("arbitrary",)),
        name="regroup",
    )(start, dst_off, npc, used, zdst, znpc, bdst, bnpc, su)


def _ffn_kernel(be_ref, nv_ref, x_ref, wg_ref, wu_ref, wd_ref, y_ref):
    del be_ref

    @pl.when(pl.program_id(0) < nv_ref[0])
    def _():
        x_lo, x_hi = _unpack_pairs(x_ref[...])
        wg = wg_ref[0].astype(BF16)
        wu = wu_ref[0].astype(BF16)
        a = _dot(x_lo, wg[:HALF]) + _dot(x_hi, wg[HALF:])
        b = _dot(x_lo, wu[:HALF]) + _dot(x_hi, wu[HALF:])
        y_ref[...] = _pack_pairs(_dot((_silu(a) * b).astype(BF16), wd_ref[0].astype(BF16)))

    @pl.when(pl.program_id(0) >= nv_ref[0])
    def _():
        y_ref[...] = jnp.zeros_like(y_ref)


def _ffn(blk_e, n_valid, xb, wg, wu, wd):
    rows = xb.shape[0]
    bk = MOE_BLOCK_ROWS
    xmap = lambda i, be, nv: (jnp.minimum(i, nv[0] - 1), 0)
    ymap = lambda i, be, nv: (i, 0)
    wmap = lambda i, be, nv: (be[i], 0, 0)
    return pl.pallas_call(
        _ffn_kernel,
        grid_spec=pltpu.PrefetchScalarGridSpec(
            num_scalar_prefetch=2,
            grid=(rows // bk,),
            in_specs=[pl.BlockSpec((bk, HALF), xmap),
                      pl.BlockSpec((1, D_MODEL, D_EXPERT), wmap),
                      pl.BlockSpec((1, D_MODEL, D_EXPERT), wmap),
                      pl.BlockSpec((1, D_EXPERT, D_MODEL), wmap)],
            out_specs=pl.BlockSpec((bk, HALF), ymap)),
        out_shape=jax.ShapeDtypeStruct((rows, HALF), I32),
        compiler_params=_params(("arbitrary",)),
        name="expert_ffn",
    )(blk_e, n_valid, xb, wg, wu, wd)


def _combine_kernel(start, src_off, npc, used, meta_ref, h_ref, p_ref, yb_ref, gp_ref, wpg_ref, wpp_ref,
                    gf_ref, o_ref, yl_ref, sel_scr, ple_scr, sem):
    tm = h_ref.shape[0]
    i = pl.program_id(0)
    piece = SUBLANES

    def rows(o, n):
        return pl.ds(pl.multiple_of(o, piece), n)

    base = i * N_EXPERTS
    pieces = _piece_loops(
        lambda e: npc[base + e],
        lambda e, off, n: pltpu.make_async_copy(yb_ref.at[rows(src_off[base + e] + off, n)],
                                                yl_ref.at[rows(start[base + e] + off, n)], sem))
    pieces(lambda cp: cp.start())

    def zero_tail(k, c):
        yl_ref[rows(used[i] + k * piece, piece), :] = jnp.zeros((piece, HALF), I32)
        return c

    lax.fori_loop(0, (LOCAL_ROWS - used[i]) // piece, zero_tail, 0)

    m = meta_ref[...]
    ci = lax.broadcasted_iota(I32, (tm, LOCAL_ROWS), 1).astype(F32)
    sel_scr[...] = (jnp.where(ci == m[:, META_P0:META_P0 + 1], m[:, META_W0:META_W0 + 1], 0.0)
                    + jnp.where(ci == m[:, META_P1:META_P1 + 1], m[:, META_W1:META_W1 + 1], 0.0)
                    ).astype(BF16)
    ple_scr[...] = _dot(p_ref[...].astype(BF16), wpp_ref[...])
    _wait_rows(used[i], lambda n: pltpu.make_async_copy(yb_ref.at[pl.ds(0, n)], yl_ref.at[pl.ds(0, n)], sem))

    y_lo, y_hi = _unpack_pairs(yl_ref[...])
    moe = jnp.concatenate([_dot(sel_scr[...], y_lo), _dot(sel_scr[...], y_hi)], axis=1)
    h = h_ref[...] + moe
    gate = _sigmoid(_dot(_rms(h, gp_ref[...]).astype(BF16), wpg_ref[...]))
    h = h + ple_scr[...] * gate
    o_ref[...] = _rms(h, gf_ref[...])


def _combine(start, src_off, npc, used, meta, h1, p2, yb, g_ple, w_pg, w_pp, g_fin):
    t = h1.shape[0]
    tm = ROW_TILE
    row = lambda n: pl.BlockSpec((tm, n), lambda i, *_: (i, 0))
    return pl.pallas_call(
        _combine_kernel,
        grid_spec=pltpu.PrefetchScalarGridSpec(
            num_scalar_prefetch=4, grid=(t // tm,),
            in_specs=[row(LANES), row(D_MODEL), row(PLE_DIM), pl.BlockSpec(memory_space=pl.ANY),
                      _const_spec((1, D_MODEL)), _const_spec((D_MODEL, D_MODEL)),
                      _const_spec((PLE_DIM, D_MODEL)), _const_spec((1, D_MODEL))],
            out_specs=row(D_MODEL),
            scratch_shapes=[pltpu.VMEM((LOCAL_ROWS, HALF), I32), pltpu.VMEM((tm, LOCAL_ROWS), BF16),
                            pltpu.VMEM((tm, D_MODEL), F32), pltpu.SemaphoreType.DMA(())]),
        out_shape=jax.ShapeDtypeStruct((t, D_MODEL), F32),
        compiler_params=_params(("arbitrary",)),
        name="combine",
    )(start, src_off, npc, used, meta, h1, p2, yb, g_ple, w_pg, w_pp, g_fin)


def _row(v, n=None):
    v = v.astype(F32).reshape(1, -1)
    if n is not None and v.shape[1] < n:
        v = jnp.pad(v, ((0, 0), (0, n - v.shape[1])))
    return v


def _layer(h2, p2, batch, seq, norm_mix_g, w_in, ssd_conv_w, ssd_conv_b, ssd_dt_bias, ssd_a_log, ssd_d,
           ssd_norm_g, w_ssd_out, conf_dw_w, conf_dw_b, conf_ln_g, conf_ln_b, w_conf_out, w_o,
           norm_ffn_g, router_group_w, router_group_b, router_expert_w, router_expert_b,
           expert_w_gate, expert_w_up, expert_w_down, norm_ple_g, w_ple_gate, w_ple_proj, final_g):
    t = batch * seq
    tm = ROW_TILE
    ntile = t // tm
    w_main = jnp.concatenate([w_in[:, :OFF_DT], w_in[:, OFF_GLU:]], axis=1).astype(BF16)
    w_dt = jnp.pad(w_in[:, OFF_DT:OFF_GLU], ((0, 0), (0, LANES - SSD_HEADS))).astype(BF16)
    conv_tab = jnp.concatenate([ssd_conv_w.astype(F32), ssd_conv_b.astype(F32)[None, :]], axis=0)
    conv_tab = jnp.pad(conv_tab, ((0, SUBLANES - conv_tab.shape[0]), (0, 0)))
    conv_tab = conv_tab.reshape(SUBLANES, XBC_SLABS, LANES).transpose(1, 0, 2)
    d_full = jnp.repeat(ssd_d.astype(F32), SSD_HEADDIM).reshape(1, D_MODEL)
    head_of_lane = jnp.arange(D_MODEL) // SSD_HEADDIM
    expand = (jnp.arange(LANES)[:, None] == head_of_lane[None, :]).astype(BF16)
    dw3 = jnp.pad(conf_dw_w.astype(F32), ((0, CONF_HALO - CONF_KERNEL), (0, 0)))
    dw3 = dw3.reshape(CONF_HALO, N_SLABS, LANES).transpose(1, 0, 2)
    wr = jnp.concatenate([router_expert_w, router_group_w], axis=1).astype(F32)
    wr = jnp.pad(wr, ((0, 0), (0, LANES - wr.shape[1])))
    wr_hi = wr.astype(BF16)
    wr_cat = jnp.concatenate([wr_hi, (wr - wr_hi.astype(F32)).astype(BF16)], axis=1)
    rb = _row(jnp.concatenate([router_expert_b, router_group_b]), LANES)

    z, xbc, dt, glu, gs, gc = _inproj(h2, _row(norm_mix_g), w_main, w_dt, _row(ssd_dt_bias, LANES))
    ys = _ssd(z, xbc, dt, gs, conv_tab, _row(ssd_a_log, LANES), d_full, expand,
              _row(ssd_norm_g), w_ssd_out.astype(BF16), batch, seq)
    h1, su, meta, tab = _merge(glu, ys, gc, h2, dw3, _row(conf_dw_b), _row(conf_ln_g), _row(conf_ln_b),
                               w_conf_out.astype(BF16), w_o.astype(BF16), _row(norm_ffn_g), wr_cat, rb,
                               batch, seq)

    bk = MOE_BLOCK_ROWS
    n_blocks = (2 * t + ntile * N_EXPERTS * (SUBLANES - 1)) // bk + N_EXPERTS + 1
    total_rows = n_blocks * bk
    rows8 = tab[:, TAB_ROWS, :N_EXPERTS].astype(I32)
    start = tab[:, TAB_START, :N_EXPERTS].astype(I32)
    per_expert = jnp.sum(rows8, axis=0)
    region = (per_expert + bk - 1) // bk * bk
    region_end = jnp.cumsum(region)
    region_start = region_end - region
    n_valid = (region_end[-1] // bk).astype(I32)
    global_off = region_start[None, :] + jnp.cumsum(rows8, axis=0) - rows8
    npc = (rows8 // SUBLANES).reshape(-1)
    start = start.reshape(-1)
    global_off = global_off.reshape(-1).astype(I32)
    blk = jnp.arange(n_blocks, dtype=I32)
    blk_e = jnp.sum((blk[:, None] * bk >= region_end[None, :]).astype(I32), axis=1)
    last_e = jnp.sum((((n_valid - 1) * bk) >= region_end).astype(I32))
    blk_e = jnp.minimum(jnp.where(blk < n_valid, blk_e, last_e), N_EXPERTS - 1).astype(I32)
    one = lambda v: jnp.reshape(v, (1,)).astype(I32)

    used = jnp.sum(rows8, axis=1).astype(I32)
    xb = _regroup(start, global_off, npc, used,
                  (region_start + per_expert).astype(I32), ((region - per_expert) // SUBLANES).astype(I32),
                  one(region_end[-1]), one(n_blocks - n_valid), su, total_rows)
    yb = _ffn(blk_e, one(n_valid), xb, expert_w_gate, expert_w_up, expert_w_down)
    return _combine(start, global_off, npc, used, meta, h1, p2, yb, _row(norm_ple_g),
                    w_ple_gate.astype(BF16), w_ple_proj.astype(BF16), final_g)


def kernel(x, p, norm_mix_g, w_in, ssd_conv_w, ssd_conv_b, ssd_dt_bias, ssd_a_log, ssd_d, ssd_norm_g,
           w_ssd_out, conf_dw_w, conf_dw_b, conf_ln_g, conf_ln_b, w_conf_out, w_o, norm_ffn_g,
           router_group_w, router_group_b, router_expert_w, router_expert_b, expert_w_gate, expert_w_up,
           expert_w_down, norm_ple_g, w_ple_gate, w_ple_proj, final_norm_g):
    batch, seq, d = x.shape
    depth = p.shape[0]
    assert d == D_MODEL and depth == 1, "single-layer block with D_MODEL features"
    assert seq % ROW_TILE == 0 and ROW_TILE % SSD_CHUNK == 0
    h2 = x.reshape(batch * seq, d)
    out = _layer(h2, p[0].reshape(batch * seq, PLE_DIM), batch, seq, norm_mix_g[0], w_in[0], ssd_conv_w[0],
                 ssd_conv_b[0], ssd_dt_bias[0], ssd_a_log[0], ssd_d[0], ssd_norm_g[0], w_ssd_out[0],
                 conf_dw_w[0], conf_dw_b[0], conf_ln_g[0], conf_ln_b[0], w_conf_out[0], w_o[0],
                 norm_ffn_g[0], router_group_w[0], router_group_b[0], router_expert_w[0],
                 router_expert_b[0], expert_w_gate[0], expert_w_up[0], expert_w_down[0], norm_ple_g[0],
                 w_ple_gate[0], w_ple_proj[0], _row(final_norm_g))
    return out.reshape(batch, seq, d)
```

```python
import jax
import jax.numpy as jnp
from jax import lax
from jax.experimental import pallas as pl
from jax.experimental.pallas import tpu as pltpu

F32 = jnp.float32
BF16 = jnp.bfloat16
I32 = jnp.int32
EPS = 1e-6

D_MODEL = 1024
PLE_DIM = 256
SSD_HEADDIM = 64
SSD_HEADS = 16
SSD_GROUPS = 2
SSD_STATE = 128
SSD_CONV = 4
SSD_CHUNK = 128
SSD_BC = SSD_GROUPS * SSD_STATE
SSD_CONV_CH = D_MODEL + 2 * SSD_BC
CONF_KERNEL = 31
N_GROUPS = 4
EXPERTS_PER_GROUP = 8
N_EXPERTS = 32
D_EXPERT = 512

OFF_XBC = D_MODEL
OFF_DT = OFF_XBC + SSD_CONV_CH
OFF_GLU = OFF_DT + SSD_HEADS
OFF_GATE = OFF_GLU + 2 * D_MODEL

LANES = 128
SUBLANES = 8
VMEM_LIMIT_BYTES = 56 * 1024 * 1024

ROW_TILE = 512
MOE_BLOCK_ROWS = 512
NEG_BIG = -1e30
HEADS_PER_SLAB = LANES // SSD_HEADDIM
N_SLABS = D_MODEL // LANES
XBC_SLABS = SSD_CONV_CH // LANES
CONF_HALO = 32
HALF = D_MODEL // 2
LOCAL_ROWS = 2 * ROW_TILE + N_EXPERTS * SUBLANES
HIGH_MASK = -65536


def _sigmoid(v):
    return jax.nn.sigmoid(v)


def _silu(v):
    return v * jax.nn.sigmoid(v)


def _rms(v, g):
    return v * lax.rsqrt(jnp.mean(v * v, axis=-1, keepdims=True) + EPS) * g


def _dot(a, b):
    return jnp.dot(a, b, preferred_element_type=F32)


def _split_bf16(v, parts):
    out = []
    r = v
    for _ in range(parts):
        h = r.astype(BF16)
        out.append(h)
        r = r - h.astype(F32)
    return out


def _pack_pairs(v, exact_bf16=False):
    r = v if exact_bf16 else v.astype(BF16).astype(F32)
    lo_bits = lax.bitcast_convert_type(r[:, :HALF], I32)
    lo = lax.shift_right_logical(lo_bits, jnp.full_like(lo_bits, 16))
    hi = lax.bitcast_convert_type(r[:, HALF:], I32) & HIGH_MASK
    return hi | lo


def _unpack_pairs(w):
    lo = lax.bitcast_convert_type(lax.shift_left(w, jnp.full_like(w, 16)), F32)
    hi = lax.bitcast_convert_type(w & HIGH_MASK, F32)
    return lo.astype(BF16), hi.astype(BF16)


def _const_spec(shape):
    nd = len(shape)
    return pl.BlockSpec(shape, lambda *_: (0,) * nd, pipeline_mode=pl.Buffered(1))


def _params(sem):
    return pltpu.CompilerParams(dimension_semantics=sem, vmem_limit_bytes=VMEM_LIMIT_BYTES)


def _inproj_kernel(x_ref, g_ref, w_ref, wdt_ref, dtb_ref,
                   z_ref, xbc_ref, dt_ref, glu_ref, gs_ref, gc_ref, u_scr):
    u_scr[...] = _rms(x_ref[...], g_ref[...]).astype(BF16)

    def seg(lo, n):
        return _dot(u_scr[...], w_ref[:, lo:lo + n])

    z_ref[...] = seg(0, D_MODEL).astype(BF16)
    blk = 512
    for j in range(SSD_CONV_CH // blk):
        xbc_ref[:, j * blk:(j + 1) * blk] = seg(D_MODEL + j * blk, blk).astype(BF16)
    o = D_MODEL + SSD_CONV_CH
    glu_ref[...] = (seg(o, D_MODEL) * _sigmoid(seg(o + D_MODEL, D_MODEL))).astype(BF16)
    gs_ref[...] = _sigmoid(seg(o + 2 * D_MODEL, D_MODEL)).astype(BF16)
    gc_ref[...] = _sigmoid(seg(o + 3 * D_MODEL, D_MODEL)).astype(BF16)
    dt_ref[...] = jax.nn.softplus(_dot(u_scr[...], wdt_ref[...]) + dtb_ref[...])


def _inproj(x2, g, w_main, w_dt, dt_bias):
    t = x2.shape[0]
    tm = ROW_TILE
    n_main = w_main.shape[1]
    row = lambda n: pl.BlockSpec((tm, n), lambda i: (i, 0))
    outs = [(D_MODEL, BF16), (SSD_CONV_CH, BF16), (LANES, F32), (D_MODEL, BF16), (D_MODEL, BF16), (D_MODEL, BF16)]
    return pl.pallas_call(
        _inproj_kernel,
        grid=(t // tm,),
        in_specs=[row(D_MODEL), _const_spec((1, D_MODEL)), _const_spec((D_MODEL, n_main)),
                  _const_spec((D_MODEL, LANES)), _const_spec((1, LANES))],
        out_specs=[row(n) for n, _ in outs],
        out_shape=[jax.ShapeDtypeStruct((t, n), d) for n, d in outs],
        scratch_shapes=[pltpu.VMEM((tm, D_MODEL), BF16)],
        compiler_params=_params(("parallel",)),
        name="inproj",
    )(x2, g, w_main, w_dt, dt_bias)


CONV_BIAS_ROW = SSD_CONV


def _ssd_kernel(z_ref, xbc_ref, dt_ref, gs_ref, cw_ref, alog_ref, dfull_ref, expand_ref,
                ng_ref, wout_ref, ys_ref, ext_scr, act_scr, state_scr, y_scr):
    tc = z_ref.shape[0]
    q = SSD_CHUNK

    @pl.when(pl.program_id(1) == 0)
    def _():
        ext_scr[:, 0:SUBLANES, :] = jnp.zeros((XBC_SLABS, SUBLANES, LANES), F32)
        state_scr[...] = jnp.zeros_like(state_scr)

    for j in range(XBC_SLABS):
        ext_scr[j, SUBLANES:SUBLANES + tc, :] = xbc_ref[:, j * LANES:(j + 1) * LANES].astype(F32)
    first = SUBLANES - (SSD_CONV - 1)

    def conv_slab(j, carry):
        for r in range(tc // q):
            acc = jnp.broadcast_to(cw_ref[j, CONV_BIAS_ROW:CONV_BIAS_ROW + 1, :], (q, LANES))
            for k in range(SSD_CONV):
                acc = acc + cw_ref[j, k:k + 1, :] * ext_scr[j, pl.ds(r * q + first + k, q), :]
            act_scr[j, r * q:(r + 1) * q, :] = _silu(acc)
        return carry

    lax.fori_loop(0, XBC_SLABS, conv_slab, 0, unroll=2)
    for j in range(XBC_SLABS):
        ext_scr[j, 0:SUBLANES, :] = ext_scr[j, tc:tc + SUBLANES, :]

    rowi = lax.broadcasted_iota(I32, (q, q), 0)
    coli = lax.broadcasted_iota(I32, (q, q), 1)
    causal = coli <= rowi
    tril = causal.astype(BF16)
    left = coli < SSD_HEADDIM
    a_neg = -jnp.exp(alog_ref[...])

    def chunk(c, carry):
        r0 = pl.multiple_of(c * q, q)
        rows = pl.ds(r0, q)
        dt = dt_ref[rows, :]
        parts = _split_bf16(dt * a_neg, 3)
        a_cs = _dot(tril, parts[0]) + _dot(tril, parts[1]) + _dot(tril, parts[2])
        a_cs_t = a_cs.T
        dt_t = dt.T
        w_t = jnp.exp(a_cs_t[:, q - 1:q] - a_cs_t) * dt_t
        cd8 = jnp.broadcast_to(jnp.exp(a_cs[q - 1:q, :]), (SUBLANES, LANES))
        cdp = _split_bf16(cd8, 3)
        cd_full = (_dot(cdp[0], expand_ref[...]) + _dot(cdp[1], expand_ref[...])
                   + _dot(cdp[2], expand_ref[...]))[0:1, :]

        slabs_per_group = N_SLABS // SSD_GROUPS
        for g in range(SSD_GROUPS):
            bg = act_scr[N_SLABS + g, rows, :]
            cg = act_scr[N_SLABS + SSD_GROUPS + g, rows, :]
            cb = lax.dot_general(cg.astype(BF16), bg.astype(BF16), (((1,), (1,)), ((), ())),
                                 preferred_element_type=F32)
            bg_t = bg.T
            for jj in range(slabs_per_group):
                j = g * slabs_per_group + jj
                ls = slice(j * LANES, (j + 1) * LANES)
                m_parts, c_parts, b_parts = [], [], []
                for hh in range(HEADS_PER_SLAB):
                    h = j * HEADS_PER_SLAB + hh
                    col = jnp.broadcast_to(a_cs[:, h:h + 1], (q, q))
                    seg = col - a_cs_t[h:h + 1, :]
                    dec = jnp.exp(jnp.where(causal, seg, NEG_BIG))
                    m_parts.append((cb * dec * dt_t[h:h + 1, :]).astype(BF16))
                    c_parts.append((cg * jnp.exp(col)).astype(BF16))
                    b_parts.append((bg_t * w_t[h:h + 1, :]).astype(BF16))
                xs = act_scr[j, rows, :]
                xs_b = xs.astype(BF16)
                st = state_scr[:, ls]
                st_b = st.astype(BF16)
                zero = jnp.zeros_like(xs_b)
                rhs_x = jnp.concatenate([jnp.where(left, xs_b, zero), jnp.where(left, zero, xs_b)], axis=0)
                rhs_s = jnp.concatenate([jnp.where(left, st_b, zero), jnp.where(left, zero, st_b)], axis=0)
                y = (_dot(jnp.concatenate(m_parts, axis=1), rhs_x)
                     + _dot(jnp.concatenate(c_parts, axis=1), rhs_s)
                     + dfull_ref[:, ls] * xs)
                y_scr[rows, ls] = y
                state_scr[:, ls] = st * cd_full[:, ls] + _dot(jnp.concatenate(b_parts, axis=1), rhs_x)
        return carry

    lax.fori_loop(0, tc // q, chunk, 0, unroll=True)

    v = y_scr[...] * _silu(z_ref[...].astype(F32))
    v = _rms(v, ng_ref[...])
    ys_ref[...] = (_dot(v.astype(BF16), wout_ref[...]) * gs_ref[...].astype(F32)).astype(BF16)


def _ssd(z, xbc, dt, gs, conv_tab, a_log, d_full, expand, norm_g, w_out, batch, seq):
    tc = ROW_TILE
    nst = seq // tc
    row = lambda n: pl.BlockSpec((tc, n), lambda b, s: (b * nst + s, 0))
    return pl.pallas_call(
        _ssd_kernel,
        grid=(batch, nst),
        in_specs=[row(D_MODEL), row(SSD_CONV_CH), row(LANES), row(D_MODEL),
                  _const_spec((XBC_SLABS, SUBLANES, LANES)),
                  _const_spec((1, LANES)), _const_spec((1, D_MODEL)), _const_spec((LANES, D_MODEL)),
                  _const_spec((1, D_MODEL)), _const_spec((D_MODEL, D_MODEL))],
        out_specs=row(D_MODEL),
        out_shape=jax.ShapeDtypeStruct((batch * seq, D_MODEL), BF16),
        scratch_shapes=[pltpu.VMEM((XBC_SLABS, SUBLANES + tc, LANES), F32),
                        pltpu.VMEM((XBC_SLABS, tc, LANES), F32),
                        pltpu.VMEM((SSD_STATE, D_MODEL), F32),
                        pltpu.VMEM((tc, D_MODEL), F32)],
        compiler_params=_params(("arbitrary", "arbitrary")),
        name="ssd",
    )(z, xbc, dt, gs, conv_tab, a_log, d_full, expand, norm_g, w_out)


META_P0, META_P1, META_W0, META_W1 = 0, 1, 2, 3
TAB_ROWS, TAB_START = 0, 1


def _route_and_sort(u, lg, su_ref, meta_ref, tab_ref):
    tm = u.shape[0]
    lane = lax.broadcasted_iota(I32, (tm, LANES), 1)
    far = jnp.int32(4 * LANES)

    def first_argmax(v):
        m = jnp.max(v, axis=-1, keepdims=True)
        return m, jnp.min(jnp.where(v == m, lane, far), axis=-1, keepdims=True)

    is_group = (lane >= N_EXPERTS) & (lane < N_EXPERTS + N_GROUPS)
    gl = jnp.where(is_group, lg, NEG_BIG)
    gmax, glane = first_argmax(gl)
    gidx = glane - N_EXPERTS
    g_prob = 1.0 / jnp.sum(jnp.where(is_group, jnp.exp(gl - gmax), 0.0), axis=-1, keepdims=True)

    in_group = (lane < N_EXPERTS) & ((lane // EXPERTS_PER_GROUP) == gidx)
    el = jnp.where(in_group, lg, NEG_BIG)
    m0, e0 = first_argmax(el)
    m1, e1 = first_argmax(jnp.where(lane == e0, NEG_BIG, el))
    ex = jnp.exp(m1 - m0)
    w0 = g_prob / (1.0 + ex)
    w1 = w0 * ex

    a0 = (lane == e0).astype(F32)
    a1 = (lane == e1).astype(F32)
    c0 = jnp.sum(a0, axis=0, keepdims=True)
    c1 = jnp.sum(a1, axis=0, keepdims=True)
    rows8 = jnp.floor((c0 + c1 + (SUBLANES - 1)) * (1.0 / SUBLANES)) * SUBLANES
    ei = lax.broadcasted_iota(I32, (LANES, LANES), 0)
    ej = lax.broadcasted_iota(I32, (LANES, LANES), 1)
    earlier_expert = (ei < ej).astype(BF16)
    start = _dot(jnp.broadcast_to(rows8, (SUBLANES, LANES)).astype(BF16), earlier_expert)[0:1, :]
    ti = lax.broadcasted_iota(I32, (tm, tm), 0)
    tj = lax.broadcasted_iota(I32, (tm, tm), 1)
    before = (tj < ti).astype(BF16)
    rank = _dot(before, jnp.concatenate([a0, a1], axis=1).astype(BF16))
    p0 = jnp.sum(a0 * (rank[:, :LANES] + start), axis=-1, keepdims=True)
    p1 = jnp.sum(a1 * (rank[:, LANES:] + start + c0), axis=-1, keepdims=True)

    meta = jnp.zeros((tm, LANES), F32)
    for pos, val in ((META_P0, p0), (META_P1, p1), (META_W0, w0), (META_W1, w1)):
        meta = jnp.where(lane == pos, val, meta)
    meta_ref[...] = meta
    sub = lax.broadcasted_iota(I32, (SUBLANES, LANES), 0)
    tab_ref[0] = jnp.where(sub == TAB_ROWS, rows8, jnp.where(sub == TAB_START, start, 0.0))

    meta_t = meta.T
    ri = lax.broadcasted_iota(I32, (LOCAL_ROWS, tm), 0).astype(F32)
    take = (ri == meta_t[META_P0:META_P0 + 1, :]) | (ri == meta_t[META_P1:META_P1 + 1, :])
    gathered = _dot(jnp.where(take, 1.0, 0.0).astype(BF16), u.astype(BF16))
    su_ref[...] = _pack_pairs(gathered, exact_bf16=True)


def _merge_kernel(glu_ref, ys_ref, gc_ref, x_ref, dw_ref, db_ref, lng_ref, lnb_ref, wc_ref, wo_ref,
                  nf_ref, wr_ref, rb_ref, h_ref, su_ref, meta_ref, tab_ref, ext_scr, c_scr):
    tm = x_ref.shape[0]

    @pl.when(pl.program_id(1) == 0)
    def _():
        ext_scr[:, 0:CONF_HALO, :] = jnp.zeros((N_SLABS, CONF_HALO, LANES), F32)

    for j in range(N_SLABS):
        ext_scr[j, CONF_HALO:CONF_HALO + tm, :] = glu_ref[:, j * LANES:(j + 1) * LANES].astype(F32)

    rc = 128
    first = CONF_HALO - (CONF_KERNEL - 1)

    def slab(j, carry):
        for r in range(tm // rc):
            acc = jnp.zeros((rc, LANES), F32)
            for k in range(CONF_KERNEL):
                acc = acc + dw_ref[j, k:k + 1, :] * ext_scr[j, pl.ds(r * rc + first + k, rc), :]
            c_scr[j, r * rc:(r + 1) * rc, :] = acc
        return carry

    lax.fori_loop(0, N_SLABS, slab, 0, unroll=2)
    for j in range(N_SLABS):
        ext_scr[j, 0:CONF_HALO, :] = ext_scr[j, tm:tm + CONF_HALO, :]

    c = jnp.concatenate([c_scr[j] for j in range(N_SLABS)], axis=1) + db_ref[...]
    mu = jnp.mean(c, axis=-1, keepdims=True)
    xc = c - mu
    yln = xc * lax.rsqrt(jnp.mean(xc * xc, axis=-1, keepdims=True) + EPS) * lng_ref[...] + lnb_ref[...]
    y_conv = _dot(_silu(yln).astype(BF16), wc_ref[...])
    mix = gc_ref[...].astype(F32) * y_conv + ys_ref[...].astype(F32)
    h = x_ref[...] + _dot(mix.astype(BF16), wo_ref[...])
    h_ref[...] = h
    u = _rms(h, nf_ref[...])
    u_hi, u_lo = _split_bf16(u, 2)
    both = _dot(u_hi, wr_ref[...])
    lg = both[:, :LANES] + both[:, LANES:] + _dot(u_lo, wr_ref[:, :LANES]) + rb_ref[...]
    _route_and_sort(u, lg, su_ref, meta_ref, tab_ref)


def _merge(glu, ys, gc, x2, dw3, dw_b, ln_g, ln_b, w_conf, w_o, nf_g, wr_cat, rb, batch, seq):
    tm = ROW_TILE
    nst = seq // tm
    t = batch * seq
    ntile = t // tm
    tile = lambda b, s: b * nst + s
    row = lambda n: pl.BlockSpec((tm, n), lambda b, s: (tile(b, s), 0))
    return pl.pallas_call(
        _merge_kernel,
        grid=(batch, nst),
        in_specs=[row(D_MODEL), row(D_MODEL), row(D_MODEL), row(D_MODEL),
                  _const_spec((N_SLABS, CONF_HALO, LANES)), _const_spec((1, D_MODEL)),
                  _const_spec((1, D_MODEL)), _const_spec((1, D_MODEL)),
                  _const_spec((D_MODEL, D_MODEL)), _const_spec((D_MODEL, D_MODEL)),
                  _const_spec((1, D_MODEL)), _const_spec((D_MODEL, 2 * LANES)), _const_spec((1, LANES))],
        out_specs=[row(D_MODEL),
                   pl.BlockSpec((LOCAL_ROWS, HALF), lambda b, s: (tile(b, s), 0)),
                   row(LANES),
                   pl.BlockSpec((1, SUBLANES, LANES), lambda b, s: (tile(b, s), 0, 0))],
        out_shape=[jax.ShapeDtypeStruct((t, D_MODEL), F32),
                   jax.ShapeDtypeStruct((ntile * LOCAL_ROWS, HALF), I32),
                   jax.ShapeDtypeStruct((t, LANES), F32),
                   jax.ShapeDtypeStruct((ntile, SUBLANES, LANES), F32)],
        scratch_shapes=[pltpu.VMEM((N_SLABS, CONF_HALO + tm, LANES), F32),
                        pltpu.VMEM((N_SLABS, tm, LANES), F32)],
        compiler_params=_params(("arbitrary", "arbitrary")),
        name="merge",
    )(glu, ys, gc, x2, dw3, dw_b, ln_g, ln_b, w_conf, w_o, nf_g, wr_cat, rb)


BIG_PIECE = MOE_BLOCK_ROWS
REGROUP_TILES = 2


LONG_COPY_ROWS = 4 * SUBLANES


def _piece_loops(n_pieces_of, copy_of):
    per_long = LONG_COPY_ROWS // SUBLANES

    def run(action):
        def chunk(e, c):
            n = n_pieces_of(e)
            n_long = lax.shift_right_logical(n, per_long.bit_length() - 1)

            def long_copy(k, cc):
                action(copy_of(e, k * LONG_COPY_ROWS, LONG_COPY_ROWS))
                return cc

            def short_copy(k, cc):
                action(copy_of(e, n_long * LONG_COPY_ROWS + k * SUBLANES, SUBLANES))
                return cc

            lax.fori_loop(0, n_long, long_copy, 0)
            lax.fori_loop(0, n & (per_long - 1), short_copy, 0)
            return c
        lax.fori_loop(0, N_EXPERTS, chunk, 0)
    return run


WAIT_SIZES = tuple(SUBLANES << b for b in range((LOCAL_ROWS // SUBLANES).bit_length() - 1, -1, -1))


def _wait_rows(n_rows, copy_of_rows):
    for size in WAIT_SIZES:
        @pl.when((n_rows & size) != 0)
        def _(size=size):
            copy_of_rows(size).wait()


def _regroup_kernel(start, dst_off, npc, used, zdst, znpc, bdst, bnpc, su_ref, xb_ref, zero_scr, sem):
    i = pl.program_id(0)

    def rows(o, n):
        return pl.ds(pl.multiple_of(o, SUBLANES), n)

    @pl.when(i == 0)
    def _():
        zero_scr[...] = jnp.zeros_like(zero_scr)
        zero_pieces = _piece_loops(
            lambda e: znpc[e],
            lambda e, off, n: pltpu.make_async_copy(zero_scr.at[pl.ds(0, n)],
                                                    xb_ref.at[rows(zdst[e] + off, n)], sem))
        zero_pieces(lambda cp: cp.start())

        def big(k):
            return pltpu.make_async_copy(zero_scr, xb_ref.at[rows(bdst[0] + k * BIG_PIECE, BIG_PIECE)], sem)

        def big_start(k, c):
            big(k).start()
            return c

        def big_wait(k, c):
            big(k).wait()
            return c

        lax.fori_loop(0, bnpc[0], big_start, 0)
        zero_pieces(lambda cp: cp.wait())
        lax.fori_loop(0, bnpc[0], big_wait, 0)

    for sub in range(REGROUP_TILES):
        tile = i * REGROUP_TILES + sub
        base = tile * N_EXPERTS
        local = sub * LOCAL_ROWS
        data_pieces = _piece_loops(
            lambda e: npc[base + e],
            lambda e, off, n: pltpu.make_async_copy(su_ref.at[rows(local + start[base + e] + off, n)],
                                                    xb_ref.at[rows(dst_off[base + e] + off, n)], sem))
        data_pieces(lambda cp: cp.start())
    for sub in range(REGROUP_TILES):
        _wait_rows(used[i * REGROUP_TILES + sub],
                   lambda n: pltpu.make_async_copy(su_ref.at[pl.ds(0, n)], xb_ref.at[pl.ds(0, n)], sem))


def _regroup(start, dst_off, npc, used, zdst, znpc, bdst, bnpc, su, out_rows):
    ntile = su.shape[0] // LOCAL_ROWS
    assert ntile % REGROUP_TILES == 0
    return pl.pallas_call(
        _regroup_kernel,
        grid_spec=pltpu.PrefetchScalarGridSpec(
            num_scalar_prefetch=8, grid=(ntile // REGROUP_TILES,),
            in_specs=[pl.BlockSpec((REGROUP_TILES * LOCAL_ROWS, HALF), lambda i, *_: (i, 0))],
            out_specs=pl.BlockSpec(memory_space=pl.ANY),
            scratch_shapes=[pltpu.VMEM((BIG_PIECE, HALF), I32), pltpu.SemaphoreType.DMA(())]),
        out_shape=jax.ShapeDtypeStruct((out_rows, HALF), I32),
        compiler_params=_params(("arbitrary",)),
        name="regroup",
    )(start, dst_off, npc, used, zdst, znpc, bdst, bnpc, su)


def _ffn_kernel(be_ref, nv_ref, x_ref, wg_ref, wu_ref, wd_ref, y_ref):
    del be_ref

    @pl.when(pl.program_id(0) < nv_ref[0])
    def _():
        x_lo, x_hi = _unpack_pairs(x_ref[...])
        wg = wg_ref[0].astype(BF16)
        wu = wu_ref[0].astype(BF16)
        a = _dot(x_lo, wg[:HALF]) + _dot(x_hi, wg[HALF:])
        b = _dot(x_lo, wu[:HALF]) + _dot(x_hi, wu[HALF:])
        y_ref[...] = _pack_pairs(_dot((_silu(a) * b).astype(BF16), wd_ref[0].astype(BF16)))

    @pl.when(pl.program_id(0) >= nv_ref[0])
    def _():
        y_ref[...] = jnp.zeros_like(y_ref)


def _ffn(blk_e, n_valid, xb, wg, wu, wd):
    rows = xb.shape[0]
    bk = MOE_BLOCK_ROWS
    xmap = lambda i, be, nv: (jnp.minimum(i, nv[0] - 1), 0)
    ymap = lambda i, be, nv: (i, 0)
    wmap = lambda i, be, nv: (be[i], 0, 0)
    return pl.pallas_call(
        _ffn_kernel,
        grid_spec=pltpu.PrefetchScalarGridSpec(
            num_scalar_prefetch=2,
            grid=(rows // bk,),
            in_specs=[pl.BlockSpec((bk, HALF), xmap),
                      pl.BlockSpec((1, D_MODEL, D_EXPERT), wmap),
                      pl.BlockSpec((1, D_MODEL, D_EXPERT), wmap),
                      pl.BlockSpec((1, D_EXPERT, D_MODEL), wmap)],
            out_specs=pl.BlockSpec((bk, HALF), ymap)),
        out_shape=jax.ShapeDtypeStruct((rows, HALF), I32),
        compiler_params=_params(("arbitrary",)),
        name="expert_ffn",
    )(blk_e, n_valid, xb, wg, wu, wd)


def _combine_kernel(start, src_off, npc, used, meta_ref, h_ref, p_ref, yb_ref, gp_ref, wpg_ref, wpp_ref,
                    gf_ref, o_ref, yl_ref, sel_scr, ple_scr, sem):
    tm = h_ref.shape[0]
    i = pl.program_id(0)
    piece = SUBLANES

    def rows(o, n):
        return pl.ds(pl.multiple_of(o, piece), n)

    base = i * N_EXPERTS
    pieces = _piece_loops(
        lambda e: npc[base + e],
        lambda e, off, n: pltpu.make_async_copy(yb_ref.at[rows(src_off[base + e] + off, n)],
                                                yl_ref.at[rows(start[base + e] + off, n)], sem))
    pieces(lambda cp: cp.start())

    def zero_tail(k, c):
        yl_ref[rows(used[i] + k * piece, piece), :] = jnp.zeros((piece, HALF), I32)
        return c

    lax.fori_loop(0, (LOCAL_ROWS - used[i]) // piece, zero_tail, 0)

    m = meta_ref[...]
    ci = lax.broadcasted_iota(I32, (tm, LOCAL_ROWS), 1).astype(F32)
    sel_scr[...] = (jnp.where(ci == m[:, META_P0:META_P0 + 1], m[:, META_W0:META_W0 + 1], 0.0)
                    + jnp.where(ci == m[:, META_P1:META_P1 + 1], m[:, META_W1:META_W1 + 1], 0.0)
                    ).astype(BF16)
    ple_scr[...] = _dot(p_ref[...].astype(BF16), wpp_ref[...])
    _wait_rows(used[i], lambda n: pltpu.make_async_copy(yb_ref.at[pl.ds(0, n)], yl_ref.at[pl.ds(0, n)], sem))

    y_lo, y_hi = _unpack_pairs(yl_ref[...])
    moe = jnp.concatenate([_dot(sel_scr[...], y_lo), _dot(sel_scr[...], y_hi)], axis=1)
    h = h_ref[...] + moe
    gate = _sigmoid(_dot(_rms(h, gp_ref[...]).astype(BF16), wpg_ref[...]))
    h = h + ple_scr[...] * gate
    o_ref[...] = _rms(h, gf_ref[...])


def _combine(start, src_off, npc, used, meta, h1, p2, yb, g_ple, w_pg, w_pp, g_fin):
    t = h1.shape[0]
    tm = ROW_TILE
    row = lambda n: pl.BlockSpec((tm, n), lambda i, *_: (i, 0))
    return pl.pallas_call(
        _combine_kernel,
        grid_spec=pltpu.PrefetchScalarGridSpec(
            num_scalar_prefetch=4, grid=(t // tm,),
            in_specs=[row(LANES), row(D_MODEL), row(PLE_DIM), pl.BlockSpec(memory_space=pl.ANY),
                      _const_spec((1, D_MODEL)), _const_spec((D_MODEL, D_MODEL)),
                      _const_spec((PLE_DIM, D_MODEL)), _const_spec((1, D_MODEL))],
            out_specs=row(D_MODEL),
            scratch_shapes=[pltpu.VMEM((LOCAL_ROWS, HALF), I32), pltpu.VMEM((tm, LOCAL_ROWS), BF16),
                            pltpu.VMEM((tm, D_MODEL), F32), pltpu.SemaphoreType.DMA(())]),
        out_shape=jax.ShapeDtypeStruct((t, D_MODEL), F32),
        compiler_params=_params(("arbitrary",)),
        name="combine",
    )(start, src_off, npc, used, meta, h1, p2, yb, g_ple, w_pg, w_pp, g_fin)


def _row(v, n=None):
    v = v.astype(F32).reshape(1, -1)
    if n is not None and v.shape[1] < n:
        v = jnp.pad(v, ((0, 0), (0, n - v.shape[1])))
    return v


def _layer(h2, p2, batch, seq, norm_mix_g, w_in, ssd_conv_w, ssd_conv_b, ssd_dt_bias, ssd_a_log, ssd_d,
           ssd_norm_g, w_ssd_out, conf_dw_w, conf_dw_b, conf_ln_g, conf_ln_b, w_conf_out, w_o,
           norm_ffn_g, router_group_w, router_group_b, router_expert_w, router_expert_b,
           expert_w_gate, expert_w_up, expert_w_down, norm_ple_g, w_ple_gate, w_ple_proj, final_g):
    t = batch * seq
    tm = ROW_TILE
    ntile = t // tm
    w_main = jnp.concatenate([w_in[:, :OFF_DT], w_in[:, OFF_GLU:]], axis=1).astype(BF16)
    w_dt = jnp.pad(w_in[:, OFF_DT:OFF_GLU], ((0, 0), (0, LANES - SSD_HEADS))).astype(BF16)
    conv_tab = jnp.concatenate([ssd_conv_w.astype(F32), ssd_conv_b.astype(F32)[None, :]], axis=0)
    conv_tab = jnp.pad(conv_tab, ((0, SUBLANES - conv_tab.shape[0]), (0, 0)))
    conv_tab = conv_tab.reshape(SUBLANES, XBC_SLABS, LANES).transpose(1, 0, 2)
    d_full = jnp.repeat(ssd_d.astype(F32), SSD_HEADDIM).reshape(1, D_MODEL)
    head_of_lane = jnp.arange(D_MODEL) // SSD_HEADDIM
    expand = (jnp.arange(LANES)[:, None] == head_of_lane[None, :]).astype(BF16)
    dw3 = jnp.pad(conf_dw_w.astype(F32), ((0, CONF_HALO - CONF_KERNEL), (0, 0)))
    dw3 = dw3.reshape(CONF_HALO, N_SLABS, LANES).transpose(1, 0, 2)
    wr = jnp.concatenate([router_expert_w, router_group_w], axis=1).astype(F32)
    wr = jnp.pad(wr, ((0, 0), (0, LANES - wr.shape[1])))
    wr_hi = wr.astype(BF16)
    wr_cat = jnp.concatenate([wr_hi, (wr - wr_hi.astype(F32)).astype(BF16)], axis=1)
    rb = _row(jnp.concatenate([router_expert_b, router_group_b]), LANES)

    z, xbc, dt, glu, gs, gc = _inproj(h2, _row(norm_mix_g), w_main, w_dt, _row(ssd_dt_bias, LANES))
    ys = _ssd(z, xbc, dt, gs, conv_tab, _row(ssd_a_log, LANES), d_full, expand,
              _row(ssd_norm_g), w_ssd_out.astype(BF16), batch, seq)
    h1, su, meta, tab = _merge(glu, ys, gc, h2, dw3, _row(conf_dw_b), _row(conf_ln_g), _row(conf_ln_b),
                               w_conf_out.astype(BF16), w_o.astype(BF16), _row(norm_ffn_g), wr_cat, rb,
                               batch, seq)

    bk = MOE_BLOCK_ROWS
    n_blocks = (2 * t + ntile * N_EXPERTS * (SUBLANES - 1)) // bk + N_EXPERTS + 1
    total_rows = n_blocks * bk
    rows8 = tab[:, TAB_ROWS, :N_EXPERTS].astype(I32)
    start = tab[:, TAB_START, :N_EXPERTS].astype(I32)
    per_expert = jnp.sum(rows8, axis=0)
    region = (per_expert + bk - 1) // bk * bk
    region_end = jnp.cumsum(region)
    region_start = region_end - region
    n_valid = (region_end[-1] // bk).astype(I32)
    global_off = region_start[None, :] + jnp.cumsum(rows8, axis=0) - rows8
    npc = (rows8 // SUBLANES).reshape(-1)
    start = start.reshape(-1)
    global_off = global_off.reshape(-1).astype(I32)
    blk = jnp.arange(n_blocks, dtype=I32)
    blk_e = jnp.sum((blk[:, None] * bk >= region_end[None, :]).astype(I32), axis=1)
    last_e = jnp.sum((((n_valid - 1) * bk) >= region_end).astype(I32))
    blk_e = jnp.minimum(jnp.where(blk < n_valid, blk_e, last_e), N_EXPERTS - 1).astype(I32)
    one = lambda v: jnp.reshape(v, (1,)).astype(I32)

    used = jnp.sum(rows8, axis=1).astype(I32)
    xb = _regroup(start, global_off, npc, used,
                  (region_start + per_expert).astype(I32), ((region - per_expert) // SUBLANES).astype(I32),
                  one(region_end[-1]), one(n_blocks - n_valid), su, total_rows)
    yb = _ffn(blk_e, one(n_valid), xb, expert_w_gate, expert_w_up, expert_w_down)
    return _combine(start, global_off, npc, used, meta, h1, p2, yb, _row(norm_ple_g),
                    w_ple_gate.astype(BF16), w_ple_proj.astype(BF16), final_g)


def kernel(x, p, norm_mix_g, w_in, ssd_conv_w, ssd_conv_b, ssd_dt_bias, ssd_a_log, ssd_d, ssd_norm_g,
           w_ssd_out, conf_dw_w, conf_dw_b, conf_ln_g, conf_ln_b, w_conf_out, w_o, norm_ffn_g,
           router_group_w, router_group_b, router_expert_w, router_expert_b, expert_w_gate, expert_w_up,
           expert_w_down, norm_ple_g, w_ple_gate, w_ple_proj, final_norm_g):
    batch, seq, d = x.shape
    depth = p.shape[0]
    assert d == D_MODEL and depth == 1, "single-layer block with D_MODEL features"
    assert seq % ROW_TILE == 0 and ROW_TILE % SSD_CHUNK == 0
    h2 = x.reshape(batch * seq, d)
    out = _layer(h2, p[0].reshape(batch * seq, PLE_DIM), batch, seq, norm_mix_g[0], w_in[0], ssd_conv_w[0],
                 ssd_conv_b[0], ssd_dt_bias[0], ssd_a_log[0], ssd_d[0], ssd_norm_g[0], w_ssd_out[0],
                 conf_dw_w[0], conf_dw_b[0], conf_ln_g[0], conf_ln_b[0], w_conf_out[0], w_o[0],
                 norm_ffn_g[0], router_group_w[0], router_group_b[0], router_expert_w[0],
                 router_expert_b[0], expert_w_gate[0], expert_w_up[0], expert_w_down[0], norm_ple_g[0],
                 w_ple_gate[0], w_ple_proj[0], _row(final_norm_g))
    return out.reshape(batch, seq, d)
```

```python
import jax
import jax.numpy as jnp
from jax import lax
from jax.experimental import pallas as pl
from jax.experimental.pallas import tpu as pltpu

F32 = jnp.float32
BF16 = jnp.bfloat16
I32 = jnp.int32
EPS = 1e-6

D_MODEL = 1024
PLE_DIM = 256
SSD_HEADDIM = 64
SSD_HEADS = 16
SSD_GROUPS = 2
SSD_STATE = 128
SSD_CONV = 4
SSD_CHUNK = 128
SSD_BC = SSD_GROUPS * SSD_STATE
SSD_CONV_CH = D_MODEL + 2 * SSD_BC
CONF_KERNEL = 31
N_GROUPS = 4
EXPERTS_PER_GROUP = 8
N_EXPERTS = 32
D_EXPERT = 512

OFF_XBC = D_MODEL
OFF_DT = OFF_XBC + SSD_CONV_CH
OFF_GLU = OFF_DT + SSD_HEADS
OFF_GATE = OFF_GLU + 2 * D_MODEL

LANES = 128
SUBLANES = 8
VMEM_LIMIT_BYTES = 56 * 1024 * 1024

ROW_TILE = 512
MOE_BLOCK_ROWS = 512
NEG_BIG = -1e30
HEADS_PER_SLAB = LANES // SSD_HEADDIM
N_SLABS = D_MODEL // LANES
XBC_SLABS = SSD_CONV_CH // LANES
CONF_HALO = 32
HALF = D_MODEL // 2
LOCAL_ROWS = 2 * ROW_TILE + N_EXPERTS * SUBLANES
HIGH_MASK = -65536


def _sigmoid(v):
    return jax.nn.sigmoid(v)


def _silu(v):
    return v * jax.nn.sigmoid(v)


def _rms(v, g):
    return v * lax.rsqrt(jnp.mean(v * v, axis=-1, keepdims=True) + EPS) * g


def _dot(a, b):
    return jnp.dot(a, b, preferred_element_type=F32)


def _split_bf16(v, parts):
    out = []
    r = v
    for _ in range(parts):
        h = r.astype(BF16)
        out.append(h)
        r = r - h.astype(F32)
    return out


def _pack_pairs(v, exact_bf16=False):
    r = v if exact_bf16 else v.astype(BF16).astype(F32)
    lo_bits = lax.bitcast_convert_type(r[:, :HALF], I32)
    lo = lax.shift_right_logical(lo_bits, jnp.full_like(lo_bits, 16))
    hi = lax.bitcast_convert_type(r[:, HALF:], I32) & HIGH_MASK
    return hi | lo


def _unpack_pairs(w):
    lo = lax.bitcast_convert_type(lax.shift_left(w, jnp.full_like(w, 16)), F32)
    hi = lax.bitcast_convert_type(w & HIGH_MASK, F32)
    return lo.astype(BF16), hi.astype(BF16)


def _const_spec(shape):
    nd = len(shape)
    return pl.BlockSpec(shape, lambda *_: (0,) * nd, pipeline_mode=pl.Buffered(1))


def _params(sem):
    return pltpu.CompilerParams(dimension_semantics=sem, vmem_limit_bytes=VMEM_LIMIT_BYTES)


def _inproj_kernel(x_ref, g_ref, w_ref, wdt_ref, dtb_ref,
                   z_ref, xbc_ref, dt_ref, glu_ref, gs_ref, gc_ref, u_scr):
    u_scr[...] = _rms(x_ref[...], g_ref[...]).astype(BF16)

    def seg(lo, n):
        return _dot(u_scr[...], w_ref[:, lo:lo + n])

    z_ref[...] = seg(0, D_MODEL).astype(BF16)
    blk = 512
    for j in range(SSD_CONV_CH // blk):
        xbc_ref[:, j * blk:(j + 1) * blk] = seg(D_MODEL + j * blk, blk).astype(BF16)
    o = D_MODEL + SSD_CONV_CH
    glu_ref[...] = (seg(o, D_MODEL) * _sigmoid(seg(o + D_MODEL, D_MODEL))).astype(BF16)
    gs_ref[...] = _sigmoid(seg(o + 2 * D_MODEL, D_MODEL)).astype(BF16)
    gc_ref[...] = _sigmoid(seg(o + 3 * D_MODEL, D_MODEL)).astype(BF16)
    dt_ref[...] = jax.nn.softplus(_dot(u_scr[...], wdt_ref[...]) + dtb_ref[...])


def _inproj(x2, g, w_main, w_dt, dt_bias):
    t = x2.shape[0]
    tm = ROW_TILE
    n_main = w_main.shape[1]
    row = lambda n: pl.BlockSpec((tm, n), lambda i: (i, 0))
    outs = [(D_MODEL, BF16), (SSD_CONV_CH, BF16), (LANES, F32), (D_MODEL, BF16), (D_MODEL, BF16), (D_MODEL, BF16)]
    return pl.pallas_call(
        _inproj_kernel,
        grid=(t // tm,),
        in_specs=[row(D_MODEL), _const_spec((1, D_MODEL)), _const_spec((D_MODEL, n_main)),
                  _const_spec((D_MODEL, LANES)), _const_spec((1, LANES))],
        out_specs=[row(n) for n, _ in outs],
        out_shape=[jax.ShapeDtypeStruct((t, n), d) for n, d in outs],
        scratch_shapes=[pltpu.VMEM((tm, D_MODEL), BF16)],
        compiler_params=_params(("parallel",)),
        name="inproj",
    )(x2, g, w_main, w_dt, dt_bias)


CONV_BIAS_ROW = SSD_CONV


def _ssd_kernel(z_ref, xbc_ref, dt_ref, gs_ref, cw_ref, alog_ref, dfull_ref, expand_ref,
                ng_ref, wout_ref, ys_ref, ext_scr, act_scr, state_scr, y_scr):
    tc = z_ref.shape[0]
    q = SSD_CHUNK

    @pl.when(pl.program_id(1) == 0)
    def _():
        ext_scr[:, 0:SUBLANES, :] = jnp.zeros((XBC_SLABS, SUBLANES, LANES), F32)
        state_scr[...] = jnp.zeros_like(state_scr)

    for j in range(XBC_SLABS):
        ext_scr[j, SUBLANES:SUBLANES + tc, :] = xbc_ref[:, j * LANES:(j + 1) * LANES].astype(F32)
    first = SUBLANES - (SSD_CONV - 1)

    def conv_slab(j, carry):
        for r in range(tc // q):
            acc = jnp.broadcast_to(cw_ref[j, CONV_BIAS_ROW:CONV_BIAS_ROW + 1, :], (q, LANES))
            for k in range(SSD_CONV):
                acc = acc + cw_ref[j, k:k + 1, :] * ext_scr[j, pl.ds(r * q + first + k, q), :]
            act_scr[j, r * q:(r + 1) * q, :] = _silu(acc)
        return carry

    lax.fori_loop(0, XBC_SLABS, conv_slab, 0, unroll=2)
    for j in range(XBC_SLABS):
        ext_scr[j, 0:SUBLANES, :] = ext_scr[j, tc:tc + SUBLANES, :]

    rowi = lax.broadcasted_iota(I32, (q, q), 0)
    coli = lax.broadcasted_iota(I32, (q, q), 1)
    causal = coli <= rowi
    tril = causal.astype(BF16)
    left = coli < SSD_HEADDIM
    a_neg = -jnp.exp(alog_ref[...])

    def chunk(c, carry):
        r0 = pl.multiple_of(c * q, q)
        rows = pl.ds(r0, q)
        dt = dt_ref[rows, :]
        parts = _split_bf16(dt * a_neg, 3)
        a_cs = _dot(tril, parts[0]) + _dot(tril, parts[1]) + _dot(tril, parts[2])
        a_cs_t = a_cs.T
        dt_t = dt.T
        w_t = jnp.exp(a_cs_t[:, q - 1:q] - a_cs_t) * dt_t
        cd8 = jnp.broadcast_to(jnp.exp(a_cs[q - 1:q, :]), (SUBLANES, LANES))
        cdp = _split_bf16(cd8, 3)
        cd_full = (_dot(cdp[0], expand_ref[...]) + _dot(cdp[1], expand_ref[...])
                   + _dot(cdp[2], expand_ref[...]))[0:1, :]

        slabs_per_group = N_SLABS // SSD_GROUPS
        for g in range(SSD_GROUPS):
            bg = act_scr[N_SLABS + g, rows, :]
            cg = act_scr[N_SLABS + SSD_GROUPS + g, rows, :]
            cb = lax.dot_general(cg.astype(BF16), bg.astype(BF16), (((1,), (1,)), ((), ())),
                                 preferred_element_type=F32)
            bg_t = bg.T
            for jj in range(slabs_per_group):
                j = g * slabs_per_group + jj
                ls = slice(j * LANES, (j + 1) * LANES)
                m_parts, c_parts, b_parts = [], [], []
                for hh in range(HEADS_PER_SLAB):
                    h = j * HEADS_PER_SLAB + hh
                    col = jnp.broadcast_to(a_cs[:, h:h + 1], (q, q))
                    seg = col - a_cs_t[h:h + 1, :]
                    dec = jnp.exp(jnp.where(causal, seg, NEG_BIG))
                    m_parts.append((cb * dec * dt_t[h:h + 1, :]).astype(BF16))
                    c_parts.append((cg * jnp.exp(col)).astype(BF16))
                    b_parts.append((bg_t * w_t[h:h + 1, :]).astype(BF16))
                xs = act_scr[j, rows, :]
                xs_b = xs.astype(BF16)
                st = state_scr[:, ls]
                st_b = st.astype(BF16)
                zero = jnp.zeros_like(xs_b)
                rhs_x = jnp.concatenate([jnp.where(left, xs_b, zero), jnp.where(left, zero, xs_b)], axis=0)
                rhs_s = jnp.concatenate([jnp.where(left, st_b, zero), jnp.where(left, zero, st_b)], axis=0)
                y = (_dot(jnp.concatenate(m_parts, axis=1), rhs_x)
                     + _dot(jnp.concatenate(c_parts, axis=1), rhs_s)
                     + dfull_ref[:, ls] * xs)
                y_scr[rows, ls] = y
                state_scr[:, ls] = st * cd_full[:, ls] + _dot(jnp.concatenate(b_parts, axis=1), rhs_x)
        return carry

    lax.fori_loop(0, tc // q, chunk, 0, unroll=True)

    v = y_scr[...] * _silu(z_ref[...].astype(F32))
    v = _rms(v, ng_ref[...])
    ys_ref[...] = (_dot(v.astype(BF16), wout_ref[...]) * gs_ref[...].astype(F32)).astype(BF16)


def _ssd(z, xbc, dt, gs, conv_tab, a_log, d_full, expand, norm_g, w_out, batch, seq):
    tc = ROW_TILE
    nst = seq // tc
    row = lambda n: pl.BlockSpec((tc, n), lambda b, s: (b * nst + s, 0))
    return pl.pallas_call(
        _ssd_kernel,
        grid=(batch, nst),
        in_specs=[row(D_MODEL), row(SSD_CONV_CH), row(LANES), row(D_MODEL),
                  _const_spec((XBC_SLABS, SUBLANES, LANES)),
                  _const_spec((1, LANES)), _const_spec((1, D_MODEL)), _const_spec((LANES, D_MODEL)),
                  _const_spec((1, D_MODEL)), _const_spec((D_MODEL, D_MODEL))],
        out_specs=row(D_MODEL),
        out_shape=jax.ShapeDtypeStruct((batch * seq, D_MODEL), BF16),
        scratch_shapes=[pltpu.VMEM((XBC_SLABS, SUBLANES + tc, LANES), F32),
                        pltpu.VMEM((XBC_SLABS, tc, LANES), F32),
                        pltpu.VMEM((SSD_STATE, D_MODEL), F32),
                        pltpu.VMEM((tc, D_MODEL), F32)],
        compiler_params=_params(("arbitrary", "arbitrary")),
        name="ssd",
    )(z, xbc, dt, gs, conv_tab, a_log, d_full, expand, norm_g, w_out)


META_P0, META_P1, META_W0, META_W1 = 0, 1, 2, 3
TAB_ROWS, TAB_START = 0, 1


def _route_and_sort(u, lg, su_ref, meta_ref, tab_ref):
    tm = u.shape[0]
    lane = lax.broadcasted_iota(I32, (tm, LANES), 1)
    far = jnp.int32(4 * LANES)

    def first_argmax(v):
        m = jnp.max(v, axis=-1, keepdims=True)
        return m, jnp.min(jnp.where(v == m, lane, far), axis=-1, keepdims=True)

    is_group = (lane >= N_EXPERTS) & (lane < N_EXPERTS + N_GROUPS)
    gl = jnp.where(is_group, lg, NEG_BIG)
    gmax, glane = first_argmax(gl)
    gidx = glane - N_EXPERTS
    g_prob = 1.0 / jnp.sum(jnp.where(is_group, jnp.exp(gl - gmax), 0.0), axis=-1, keepdims=True)

    in_group = (lane < N_EXPERTS) & ((lane // EXPERTS_PER_GROUP) == gidx)
    el = jnp.where(in_group, lg, NEG_BIG)
    m0, e0 = first_argmax(el)
    m1, e1 = first_argmax(jnp.where(lane == e0, NEG_BIG, el))
    ex = jnp.exp(m1 - m0)
    w0 = g_prob / (1.0 + ex)
    w1 = w0 * ex

    a0 = (lane == e0).astype(F32)
    a1 = (lane == e1).astype(F32)
    c0 = jnp.sum(a0, axis=0, keepdims=True)
    c1 = jnp.sum(a1, axis=0, keepdims=True)
    rows8 = jnp.floor((c0 + c1 + (SUBLANES - 1)) * (1.0 / SUBLANES)) * SUBLANES
    ei = lax.broadcasted_iota(I32, (LANES, LANES), 0)
    ej = lax.broadcasted_iota(I32, (LANES, LANES), 1)
    earlier_expert = (ei < ej).astype(BF16)
    start = _dot(jnp.broadcast_to(rows8, (SUBLANES, LANES)).astype(BF16), earlier_expert)[0:1, :]
    ti = lax.broadcasted_iota(I32, (tm, tm), 0)
    tj = lax.broadcasted_iota(I32, (tm, tm), 1)
    before = (tj < ti).astype(BF16)
    rank = _dot(before, jnp.concatenate([a0, a1], axis=1).astype(BF16))
    p0 = jnp.sum(a0 * (rank[:, :LANES] + start), axis=-1, keepdims=True)
    p1 = jnp.sum(a1 * (rank[:, LANES:] + start + c0), axis=-1, keepdims=True)

    meta = jnp.zeros((tm, LANES), F32)
    for pos, val in ((META_P0, p0), (META_P1, p1), (META_W0, w0), (META_W1, w1)):
        meta = jnp.where(lane == pos, val, meta)
    meta_ref[...] = meta
    sub = lax.broadcasted_iota(I32, (SUBLANES, LANES), 0)
    tab_ref[0] = jnp.where(sub == TAB_ROWS, rows8, jnp.where(sub == TAB_START, start, 0.0))

    meta_t = meta.T
    ri = lax.broadcasted_iota(I32, (LOCAL_ROWS, tm), 0).astype(F32)
    take = (ri == meta_t[META_P0:META_P0 + 1, :]) | (ri == meta_t[META_P1:META_P1 + 1, :])
    gathered = _dot(jnp.where(take, 1.0, 0.0).astype(BF16), u.astype(BF16))
    su_ref[...] = _pack_pairs(gathered, exact_bf16=True)


def _merge_kernel(glu_ref, ys_ref, gc_ref, x_ref, dw_ref, db_ref, lng_ref, lnb_ref, wc_ref, wo_ref,
                  nf_ref, wr_ref, rb_ref, h_ref, su_ref, meta_ref, tab_ref, ext_scr, c_scr):
    tm = x_ref.shape[0]

    @pl.when(pl.program_id(1) == 0)
    def _():
        ext_scr[:, 0:CONF_HALO, :] = jnp.zeros((N_SLABS, CONF_HALO, LANES), F32)

    for j in range(N_SLABS):
        ext_scr[j, CONF_HALO:CONF_HALO + tm, :] = glu_ref[:, j * LANES:(j + 1) * LANES].astype(F32)

    rc = 128
    first = CONF_HALO - (CONF_KERNEL - 1)

    def slab(j, carry):
        for r in range(tm // rc):
            acc = jnp.zeros((rc, LANES), F32)
            for k in range(CONF_KERNEL):
                acc = acc + dw_ref[j, k:k + 1, :] * ext_scr[j, pl.ds(r * rc + first + k, rc), :]
            c_scr[j, r * rc:(r + 1) * rc, :] = acc
        return carry

    lax.fori_loop(0, N_SLABS, slab, 0, unroll=2)
    for j in range(N_SLABS):
        ext_scr[j, 0:CONF_HALO, :] = ext_scr[j, tm:tm + CONF_HALO, :]

    c = jnp.concatenate([c_scr[j] for j in range(N_SLABS)], axis=1) + db_ref[...]
    mu = jnp.mean(c, axis=-1, keepdims=True)
    xc = c - mu
    yln = xc * lax.rsqrt(jnp.mean(xc * xc, axis=-1, keepdims=True) + EPS) * lng_ref[...] + lnb_ref[...]
    y_conv = _dot(_silu(yln).astype(BF16), wc_ref[...])
    mix = gc_ref[...].astype(F32) * y_conv + ys_ref[...].astype(F32)
    h = x_ref[...] + _dot(mix.astype(BF16), wo_ref[...])
    h_ref[...] = h
    u = _rms(h, nf_ref[...])
    u_hi, u_lo = _split_bf16(u, 2)
    both = _dot(u_hi, wr_ref[...])
    lg = both[:, :LANES] + both[:, LANES:] + _dot(u_lo, wr_ref[:, :LANES]) + rb_ref[...]
    _route_and_sort(u, lg, su_ref, meta_ref, tab_ref)


def _merge(glu, ys, gc, x2, dw3, dw_b, ln_g, ln_b, w_conf, w_o, nf_g, wr_cat, rb, batch, seq):
    tm = ROW_TILE
    nst = seq // tm
    t = batch * seq
    ntile = t // tm
    tile = lambda b, s: b * nst + s
    row = lambda n: pl.BlockSpec((tm, n), lambda b, s: (tile(b, s), 0))
    return pl.pallas_call(
        _merge_kernel,
        grid=(batch, nst),
        in_specs=[row(D_MODEL), row(D_MODEL), row(D_MODEL), row(D_MODEL),
                  _const_spec((N_SLABS, CONF_HALO, LANES)), _const_spec((1, D_MODEL)),
                  _const_spec((1, D_MODEL)), _const_spec((1, D_MODEL)),
                  _const_spec((D_MODEL, D_MODEL)), _const_spec((D_MODEL, D_MODEL)),
                  _const_spec((1, D_MODEL)), _const_spec((D_MODEL, 2 * LANES)), _const_spec((1, LANES))],
        out_specs=[row(D_MODEL),
                   pl.BlockSpec((LOCAL_ROWS, HALF), lambda b, s: (tile(b, s), 0)),
                   row(LANES),
                   pl.BlockSpec((1, SUBLANES, LANES), lambda b, s: (tile(b, s), 0, 0))],
        out_shape=[jax.ShapeDtypeStruct((t, D_MODEL), F32),
                   jax.ShapeDtypeStruct((ntile * LOCAL_ROWS, HALF), I32),
                   jax.ShapeDtypeStruct((t, LANES), F32),
                   jax.ShapeDtypeStruct((ntile, SUBLANES, LANES), F32)],
        scratch_shapes=[pltpu.VMEM((N_SLABS, CONF_HALO + tm, LANES), F32),
                        pltpu.VMEM((N_SLABS, tm, LANES), F32)],
        compiler_params=_params(("arbitrary", "arbitrary")),
        name="merge",
    )(glu, ys, gc, x2, dw3, dw_b, ln_g, ln_b, w_conf, w_o, nf_g, wr_cat, rb)


BIG_PIECE = MOE_BLOCK_ROWS
REGROUP_TILES = 4


LONG_COPY_ROWS = 4 * SUBLANES


def _piece_loops(n_pieces_of, copy_of):
    per_long = LONG_COPY_ROWS // SUBLANES

    def run(action):
        def chunk(e, c):
            n = n_pieces_of(e)
            n_long = lax.shift_right_logical(n, per_long.bit_length() - 1)

            def long_copy(k, cc):
                action(copy_of(e, k * LONG_COPY_ROWS, LONG_COPY_ROWS))
                return cc

            def short_copy(k, cc):
                action(copy_of(e, n_long * LONG_COPY_ROWS + k * SUBLANES, SUBLANES))
                return cc

            lax.fori_loop(0, n_long, long_copy, 0)
            lax.fori_loop(0, n & (per_long - 1), short_copy, 0)
            return c
        lax.fori_loop(0, N_EXPERTS, chunk, 0)
    return run


WAIT_SIZES = tuple(SUBLANES << b for b in range((LOCAL_ROWS // SUBLANES).bit_length() - 1, -1, -1))


def _wait_rows(n_rows, copy_of_rows):
    for size in WAIT_SIZES:
        @pl.when((n_rows & size) != 0)
        def _(size=size):
            copy_of_rows(size).wait()


def _regroup_kernel(start, dst_off, npc, used, zdst, znpc, bdst, bnpc, su_ref, xb_ref, zero_scr, sem):
    i = pl.program_id(0)

    def rows(o, n):
        return pl.ds(pl.multiple_of(o, SUBLANES), n)

    @pl.when(i == 0)
    def _():
        zero_scr[...] = jnp.zeros_like(zero_scr)
        zero_pieces = _piece_loops(
            lambda e: znpc[e],
            lambda e, off, n: pltpu.make_async_copy(zero_scr.at[pl.ds(0, n)],
                                                    xb_ref.at[rows(zdst[e] + off, n)], sem))
        zero_pieces(lambda cp: cp.start())

        def big(k):
            return pltpu.make_async_copy(zero_scr, xb_ref.at[rows(bdst[0] + k * BIG_PIECE, BIG_PIECE)], sem)

        def big_start(k, c):
            big(k).start()
            return c

        def big_wait(k, c):
            big(k).wait()
            return c

        lax.fori_loop(0, bnpc[0], big_start, 0)
        zero_pieces(lambda cp: cp.wait())
        lax.fori_loop(0, bnpc[0], big_wait, 0)

    for sub in range(REGROUP_TILES):
        tile = i * REGROUP_TILES + sub
        base = tile * N_EXPERTS
        local = sub * LOCAL_ROWS
        data_pieces = _piece_loops(
            lambda e: npc[base + e],
            lambda e, off, n: pltpu.make_async_copy(su_ref.at[rows(local + start[base + e] + off, n)],
                                                    xb_ref.at[rows(dst_off[base + e] + off, n)], sem))
        data_pieces(lambda cp: cp.start())
    for sub in range(REGROUP_TILES):
        _wait_rows(used[i * REGROUP_TILES + sub],
                   lambda n: pltpu.make_async_copy(su_ref.at[pl.ds(0, n)], xb_ref.at[pl.ds(0, n)], sem))


def _regroup(start, dst_off, npc, used, zdst, znpc, bdst, bnpc, su, out_rows):
    ntile = su.shape[0] // LOCAL_ROWS
    assert ntile % REGROUP_TILES == 0
    return pl.pallas_call(
        _regroup_kernel,
        grid_spec=pltpu.PrefetchScalarGridSpec(
            num_scalar_prefetch=8, grid=(ntile // REGROUP_TILES,),
            in_specs=[pl.BlockSpec((REGROUP_TILES * LOCAL_ROWS, HALF), lambda i, *_: (i, 0))],
            out_specs=pl.BlockSpec(memory_space=pl.ANY),
            scratch_shapes=[pltpu.VMEM((BIG_PIECE, HALF), I32), pltpu.SemaphoreType.DMA(())]),
        out_shape=jax.ShapeDtypeStruct((out_rows, HALF), I32),
        compiler_params=_params(("arbitrary",)),
        name="regroup",
    )(start, dst_off, npc, used, zdst, znpc, bdst, bnpc, su)


def _ffn_kernel(be_ref, nv_ref, x_ref, wg_ref, wu_ref, wd_ref, y_ref):
    del be_ref

    @pl.when(pl.program_id(0) < nv_ref[0])
    def _():
        x_lo, x_hi = _unpack_pairs(x_ref[...])
        wg = wg_ref[0].astype(BF16)
        wu = wu_ref[0].astype(BF16)
        a = _dot(x_lo, wg[:HALF]) + _dot(x_hi, wg[HALF:])
        b = _dot(x_lo, wu[:HALF]) + _dot(x_hi, wu[HALF:])
        y_ref[...] = _pack_pairs(_dot((_silu(a) * b).astype(BF16), wd_ref[0].astype(BF16)))

    @pl.when(pl.program_id(0) >= nv_ref[0])
    def _():
        y_ref[...] = jnp.zeros_like(y_ref)


def _ffn(blk_e, n_valid, xb, wg, wu, wd):
    rows = xb.shape[0]
    bk = MOE_BLOCK_ROWS
    xmap = lambda i, be, nv: (jnp.minimum(i, nv[0] - 1), 0)
    ymap = lambda i, be, nv: (i, 0)
    wmap = lambda i, be, nv: (be[i], 0, 0)
    return pl.pallas_call(
        _ffn_kernel,
        grid_spec=pltpu.PrefetchScalarGridSpec(
            num_scalar_prefetch=2,
            grid=(rows // bk,),
            in_specs=[pl.BlockSpec((bk, HALF), xmap),
                      pl.BlockSpec((1, D_MODEL, D_EXPERT), wmap),
                      pl.BlockSpec((1, D_MODEL, D_EXPERT), wmap),
                      pl.BlockSpec((1, D_EXPERT, D_MODEL), wmap)],
            out_specs=pl.BlockSpec((bk, HALF), ymap)),
        out_shape=jax.ShapeDtypeStruct((rows, HALF), I32),
        compiler_params=_params(("arbitrary",)),
        name="expert_ffn",
    )(blk_e, n_valid, xb, wg, wu, wd)


def _combine_kernel(start, src_off, npc, used, meta_ref, h_ref, p_ref, yb_ref, gp_ref, wpg_ref, wpp_ref,
                    gf_ref, o_ref, yl_ref, sel_scr, ple_scr, sem):
    tm = h_ref.shape[0]
    i = pl.program_id(0)
    piece = SUBLANES

    def rows(o, n):
        return pl.ds(pl.multiple_of(o, piece), n)

    base = i * N_EXPERTS
    pieces = _piece_loops(
        lambda e: npc[base + e],
        lambda e, off, n: pltpu.make_async_copy(yb_ref.at[rows(src_off[base + e] + off, n)],
                                                yl_ref.at[rows(start[base + e] + off, n)], sem))
    pieces(lambda cp: cp.start())

    def zero_tail(k, c):
        yl_ref[rows(used[i] + k * piece, piece), :] = jnp.zeros((piece, HALF), I32)
        return c

    lax.fori_loop(0, (LOCAL_ROWS - used[i]) // piece, zero_tail, 0)

    m = meta_ref[...]
    ci = lax.broadcasted_iota(I32, (tm, LOCAL_ROWS), 1).astype(F32)
    sel_scr[...] = (jnp.where(ci == m[:, META_P0:META_P0 + 1], m[:, META_W0:META_W0 + 1], 0.0)
                    + jnp.where(ci == m[:, META_P1:META_P1 + 1], m[:, META_W1:META_W1 + 1], 0.0)
                    ).astype(BF16)
    ple_scr[...] = _dot(p_ref[...].astype(BF16), wpp_ref[...])
    _wait_rows(used[i], lambda n: pltpu.make_async_copy(yb_ref.at[pl.ds(0, n)], yl_ref.at[pl.ds(0, n)], sem))

    y_lo, y_hi = _unpack_pairs(yl_ref[...])
    moe = jnp.concatenate([_dot(sel_scr[...], y_lo), _dot(sel_scr[...], y_hi)], axis=1)
    h = h_ref[...] + moe
    gate = _sigmoid(_dot(_rms(h, gp_ref[...]).astype(BF16), wpg_ref[...]))
    h = h + ple_scr[...] * gate
    o_ref[...] = _rms(h, gf_ref[...])


def _combine(start, src_off, npc, used, meta, h1, p2, yb, g_ple, w_pg, w_pp, g_fin):
    t = h1.shape[0]
    tm = ROW_TILE
    row = lambda n: pl.BlockSpec((tm, n), lambda i, *_: (i, 0))
    return pl.pallas_call(
        _combine_kernel,
        grid_spec=pltpu.PrefetchScalarGridSpec(
            num_scalar_prefetch=4, grid=(t // tm,),
            in_specs=[row(LANES), row(D_MODEL), row(PLE_DIM), pl.BlockSpec(memory_space=pl.ANY),
                      _const_spec((1, D_MODEL)), _const_spec((D_MODEL, D_MODEL)),
                      _const_spec((PLE_DIM, D_MODEL)), _const_spec((1, D_MODEL))],
            out_specs=row(D_MODEL),
            scratch_shapes=[pltpu.VMEM((LOCAL_ROWS, HALF), I32), pltpu.VMEM((tm, LOCAL_ROWS), BF16),
                            pltpu.VMEM((tm, D_MODEL), F32), pltpu.SemaphoreType.DMA(())]),
        out_shape=jax.ShapeDtypeStruct((t, D_MODEL), F32),
        compiler_params=_params(("arbitrary",)),
        name="combine",
    )(start, src_off, npc, used, meta, h1, p2, yb, g_ple, w_pg, w_pp, g_fin)


def _row(v, n=None):
    v = v.astype(F32).reshape(1, -1)
    if n is not None and v.shape[1] < n:
        v = jnp.pad(v, ((0, 0), (0, n - v.shape[1])))
    return v


def _layer(h2, p2, batch, seq, norm_mix_g, w_in, ssd_conv_w, ssd_conv_b, ssd_dt_bias, ssd_a_log, ssd_d,
           ssd_norm_g, w_ssd_out, conf_dw_w, conf_dw_b, conf_ln_g, conf_ln_b, w_conf_out, w_o,
           norm_ffn_g, router_group_w, router_group_b, router_expert_w, router_expert_b,
           expert_w_gate, expert_w_up, expert_w_down, norm_ple_g, w_ple_gate, w_ple_proj, final_g):
    t = batch * seq
    tm = ROW_TILE
    ntile = t // tm
    w_main = jnp.concatenate([w_in[:, :OFF_DT], w_in[:, OFF_GLU:]], axis=1).astype(BF16)
    w_dt = jnp.pad(w_in[:, OFF_DT:OFF_GLU], ((0, 0), (0, LANES - SSD_HEADS))).astype(BF16)
    conv_tab = jnp.concatenate([ssd_conv_w.astype(F32), ssd_conv_b.astype(F32)[None, :]], axis=0)
    conv_tab = jnp.pad(conv_tab, ((0, SUBLANES - conv_tab.shape[0]), (0, 0)))
    conv_tab = conv_tab.reshape(SUBLANES, XBC_SLABS, LANES).transpose(1, 0, 2)
    d_full = jnp.repeat(ssd_d.astype(F32), SSD_HEADDIM).reshape(1, D_MODEL)
    head_of_lane = jnp.arange(D_MODEL) // SSD_HEADDIM
    expand = (jnp.arange(LANES)[:, None] == head_of_lane[None, :]).astype(BF16)
    dw3 = jnp.pad(conf_dw_w.astype(F32), ((0, CONF_HALO - CONF_KERNEL), (0, 0)))
    dw3 = dw3.reshape(CONF_HALO, N_SLABS, LANES).transpose(1, 0, 2)
    wr = jnp.concatenate([router_expert_w, router_group_w], axis=1).astype(F32)
    wr = jnp.pad(wr, ((0, 0), (0, LANES - wr.shape[1])))
    wr_hi = wr.astype(BF16)
    wr_cat = jnp.concatenate([wr_hi, (wr - wr_hi.astype(F32)).astype(BF16)], axis=1)
    rb = _row(jnp.concatenate([router_expert_b, router_group_b]), LANES)

    z, xbc, dt, glu, gs, gc = _inproj(h2, _row(norm_mix_g), w_main, w_dt, _row(ssd_dt_bias, LANES))
    ys = _ssd(z, xbc, dt, gs, conv_tab, _row(ssd_a_log, LANES), d_full, expand,
              _row(ssd_norm_g), w_ssd_out.astype(BF16), batch, seq)
    h1, su, meta, tab = _merge(glu, ys, gc, h2, dw3, _row(conf_dw_b), _row(conf_ln_g), _row(conf_ln_b),
                               w_conf_out.astype(BF16), w_o.astype(BF16), _row(norm_ffn_g), wr_cat, rb,
                               batch, seq)

    bk = MOE_BLOCK_ROWS
    n_blocks = (2 * t + ntile * N_EXPERTS * (SUBLANES - 1)) // bk + N_EXPERTS + 1
    total_rows = n_blocks * bk
    rows8 = tab[:, TAB_ROWS, :N_EXPERTS].astype(I32)
    start = tab[:, TAB_START, :N_EXPERTS].astype(I32)
    per_expert = jnp.sum(rows8, axis=0)
    region = (per_expert + bk - 1) // bk * bk
    region_end = jnp.cumsum(region)
    region_start = region_end - region
    n_valid = (region_end[-1] // bk).astype(I32)
    global_off = region_start[None, :] + jnp.cumsum(rows8, axis=0) - rows8
    npc = (rows8 // SUBLANES).reshape(-1)
    start = start.reshape(-1)
    global_off = global_off.reshape(-1).astype(I32)
    blk = jnp.arange(n_blocks, dtype=I32)
    blk_e = jnp.sum((blk[:, None] * bk >= region_end[None, :]).astype(I32), axis=1)
    last_e = jnp.sum((((n_valid - 1) * bk) >= region_end).astype(I32))
    blk_e = jnp.minimum(jnp.where(blk < n_valid, blk_e, last_e), N_EXPERTS - 1).astype(I32)
    one = lambda v: jnp.reshape(v, (1,)).astype(I32)

    used = jnp.sum(rows8, axis=1).astype(I32)
    xb = _regroup(start, global_off, npc, used,
                  (region_start + per_expert).astype(I32), ((region - per_expert) // SUBLANES).astype(I32),
                  one(region_end[-1]), one(n_blocks - n_valid), su, total_rows)
    yb = _ffn(blk_e, one(n_valid), xb, expert_w_gate, expert_w_up, expert_w_down)
    return _combine(start, global_off, npc, used, meta, h1, p2, yb, _row(norm_ple_g),
                    w_ple_gate.astype(BF16), w_ple_proj.astype(BF16), final_g)


def kernel(x, p, norm_mix_g, w_in, ssd_conv_w, ssd_conv_b, ssd_dt_bias, ssd_a_log, ssd_d, ssd_norm_g,
           w_ssd_out, conf_dw_w, conf_dw_b, conf_ln_g, conf_ln_b, w_conf_out, w_o, norm_ffn_g,
           router_group_w, router_group_b, router_expert_w, router_expert_b, expert_w_gate, expert_w_up,
           expert_w_down, norm_ple_g, w_ple_gate, w_ple_proj, final_norm_g):
    batch, seq, d = x.shape
    depth = p.shape[0]
    assert d == D_MODEL and depth == 1, "single-layer block with D_MODEL features"
    assert seq % ROW_TILE == 0 and ROW_TILE % SSD_CHUNK == 0
    h2 = x.reshape(batch * seq, d)
    out = _layer(h2, p[0].reshape(batch * seq, PLE_DIM), batch, seq, norm_mix_g[0], w_in[0], ssd_conv_w[0],
                 ssd_conv_b[0], ssd_dt_bias[0], ssd_a_log[0], ssd_d[0], ssd_norm_g[0], w_ssd_out[0],
                 conf_dw_w[0], conf_dw_b[0], conf_ln_g[0], conf_ln_b[0], w_conf_out[0], w_o[0],
                 norm_ffn_g[0], router_group_w[0], router_group_b[0], router_expert_w[0],
                 router_expert_b[0], expert_w_gate[0], expert_w_up[0], expert_w_down[0], norm_ple_g[0],
                 w_ple_gate[0], w_ple_proj[0], _row(final_norm_g))
    return out.reshape(batch, seq, d)
```

```python
import jax
import jax.numpy as jnp
from jax import lax
from jax.experimental import pallas as pl
from jax.experimental.pallas import tpu as pltpu

F32 = jnp.float32
BF16 = jnp.bfloat16
I32 = jnp.int32
EPS = 1e-6

D_MODEL = 1024
PLE_DIM = 256
SSD_HEADDIM = 64
SSD_HEADS = 16
SSD_GROUPS = 2
SSD_STATE = 128
SSD_CONV = 4
SSD_CHUNK = 128
SSD_BC = SSD_GROUPS * SSD_STATE
SSD_CONV_CH = D_MODEL + 2 * SSD_BC
CONF_KERNEL = 31
N_GROUPS = 4
EXPERTS_PER_GROUP = 8
N_EXPERTS = 32
D_EXPERT = 512

OFF_XBC = D_MODEL
OFF_DT = OFF_XBC + SSD_CONV_CH
OFF_GLU = OFF_DT + SSD_HEADS
OFF_GATE = OFF_GLU + 2 * D_MODEL

LANES = 128
SUBLANES = 8
VMEM_LIMIT_BYTES = 56 * 1024 * 1024

ROW_TILE = 512
MOE_BLOCK_ROWS = 512
NEG_BIG = -1e30
HEADS_PER_SLAB = LANES // SSD_HEADDIM
N_SLABS = D_MODEL // LANES
XBC_SLABS = SSD_CONV_CH // LANES
CONF_HALO = 32
HALF = D_MODEL // 2
LOCAL_ROWS = 2 * ROW_TILE + N_EXPERTS * SUBLANES
HIGH_MASK = -65536


def _sigmoid(v):
    return jax.nn.sigmoid(v)


def _silu(v):
    return v * jax.nn.sigmoid(v)


def _rms(v, g):
    return v * lax.rsqrt(jnp.mean(v * v, axis=-1, keepdims=True) + EPS) * g


def _dot(a, b):
    return jnp.dot(a, b, preferred_element_type=F32)


def _split_bf16(v, parts):
    out = []
    r = v
    for _ in range(parts):
        h = r.astype(BF16)
        out.append(h)
        r = r - h.astype(F32)
    return out


def _pack_pairs(v, exact_bf16=False):
    r = v if exact_bf16 else v.astype(BF16).astype(F32)
    lo_bits = lax.bitcast_convert_type(r[:, :HALF], I32)
    lo = lax.shift_right_logical(lo_bits, jnp.full_like(lo_bits, 16))
    hi = lax.bitcast_convert_type(r[:, HALF:], I32) & HIGH_MASK
    return hi | lo


def _unpack_pairs(w):
    lo = lax.bitcast_convert_type(lax.shift_left(w, jnp.full_like(w, 16)), F32)
    hi = lax.bitcast_convert_type(w & HIGH_MASK, F32)
    return lo.astype(BF16), hi.astype(BF16)


def _const_spec(shape):
    nd = len(shape)
    return pl.BlockSpec(shape, lambda *_: (0,) * nd, pipeline_mode=pl.Buffered(1))


def _params(sem):
    return pltpu.CompilerParams(dimension_semantics=sem, vmem_limit_bytes=VMEM_LIMIT_BYTES)


def _inproj_kernel(x_ref, g_ref, w_ref, wdt_ref, dtb_ref,
                   z_ref, xbc_ref, dt_ref, glu_ref, gs_ref, gc_ref, u_scr):
    u_scr[...] = _rms(x_ref[...], g_ref[...]).astype(BF16)

    def seg(lo, n):
        return _dot(u_scr[...], w_ref[:, lo:lo + n])

    z_ref[...] = seg(0, D_MODEL).astype(BF16)
    blk = 512
    for j in range(SSD_CONV_CH // blk):
        xbc_ref[:, j * blk:(j + 1) * blk] = seg(D_MODEL + j * blk, blk).astype(BF16)
    o = D_MODEL + SSD_CONV_CH
    glu_ref[...] = (seg(o, D_MODEL) * _sigmoid(seg(o + D_MODEL, D_MODEL))).astype(BF16)
    gs_ref[...] = _sigmoid(seg(o + 2 * D_MODEL, D_MODEL)).astype(BF16)
    gc_ref[...] = _sigmoid(seg(o + 3 * D_MODEL, D_MODEL)).astype(BF16)
    dt_ref[...] = jax.nn.softplus(_dot(u_scr[...], wdt_ref[...]) + dtb_ref[...])


def _inproj(x2, g, w_main, w_dt, dt_bias):
    t = x2.shape[0]
    tm = ROW_TILE
    n_main = w_main.shape[1]
    row = lambda n: pl.BlockSpec((tm, n), lambda i: (i, 0))
    outs = [(D_MODEL, BF16), (SSD_CONV_CH, BF16), (LANES, F32), (D_MODEL, BF16), (D_MODEL, BF16), (D_MODEL, BF16)]
    return pl.pallas_call(
        _inproj_kernel,
        grid=(t // tm,),
        in_specs=[row(D_MODEL), _const_spec((1, D_MODEL)), _const_spec((D_MODEL, n_main)),
                  _const_spec((D_MODEL, LANES)), _const_spec((1, LANES))],
        out_specs=[row(n) for n, _ in outs],
        out_shape=[jax.ShapeDtypeStruct((t, n), d) for n, d in outs],
        scratch_shapes=[pltpu.VMEM((tm, D_MODEL), BF16)],
        compiler_params=_params(("parallel",)),
        name="inproj",
    )(x2, g, w_main, w_dt, dt_bias)


CONV_BIAS_ROW = SSD_CONV


def _ssd_kernel(z_ref, xbc_ref, dt_ref, gs_ref, cw_ref, alog_ref, dfull_ref, expand_ref,
                ng_ref, wout_ref, ys_ref, ext_scr, act_scr, state_scr, y_scr):
    tc = z_ref.shape[0]
    q = SSD_CHUNK

    @pl.when(pl.program_id(1) == 0)
    def _():
        ext_scr[:, 0:SUBLANES, :] = jnp.zeros((XBC_SLABS, SUBLANES, LANES), F32)
        state_scr[...] = jnp.zeros_like(state_scr)

    for j in range(XBC_SLABS):
        ext_scr[j, SUBLANES:SUBLANES + tc, :] = xbc_ref[:, j * LANES:(j + 1) * LANES].astype(F32)
    first = SUBLANES - (SSD_CONV - 1)

    def conv_slab(j, carry):
        for r in range(tc // q):
            acc = jnp.broadcast_to(cw_ref[j, CONV_BIAS_ROW:CONV_BIAS_ROW + 1, :], (q, LANES))
            for k in range(SSD_CONV):
                acc = acc + cw_ref[j, k:k + 1, :] * ext_scr[j, pl.ds(r * q + first + k, q), :]
            act_scr[j, r * q:(r + 1) * q, :] = _silu(acc)
        return carry

    lax.fori_loop(0, XBC_SLABS, conv_slab, 0, unroll=2)
    for j in range(XBC_SLABS):
        ext_scr[j, 0:SUBLANES, :] = ext_scr[j, tc:tc + SUBLANES, :]

    rowi = lax.broadcasted_iota(I32, (q, q), 0)
    coli = lax.broadcasted_iota(I32, (q, q), 1)
    causal = coli <= rowi
    tril = causal.astype(BF16)
    left = coli < SSD_HEADDIM
    a_neg = -jnp.exp(alog_ref[...])

    def chunk(c, carry):
        r0 = pl.multiple_of(c * q, q)
        rows = pl.ds(r0, q)
        dt = dt_ref[rows, :]
        parts = _split_bf16(dt * a_neg, 3)
        a_cs = _dot(tril, parts[0]) + _dot(tril, parts[1]) + _dot(tril, parts[2])
        a_cs_t = a_cs.T
        dt_t = dt.T
        w_t = jnp.exp(a_cs_t[:, q - 1:q] - a_cs_t) * dt_t
        cd8 = jnp.broadcast_to(jnp.exp(a_cs[q - 1:q, :]), (SUBLANES, LANES))
        cdp = _split_bf16(cd8, 3)
        cd_full = (_dot(cdp[0], expand_ref[...]) + _dot(cdp[1], expand_ref[...])
                   + _dot(cdp[2], expand_ref[...]))[0:1, :]

        slabs_per_group = N_SLABS // SSD_GROUPS
        for g in range(SSD_GROUPS):
            bg = act_scr[N_SLABS + g, rows, :]
            cg = act_scr[N_SLABS + SSD_GROUPS + g, rows, :]
            cb = lax.dot_general(cg.astype(BF16), bg.astype(BF16), (((1,), (1,)), ((), ())),
                                 preferred_element_type=F32)
            bg_t = bg.T
            for jj in range(slabs_per_group):
                j = g * slabs_per_group + jj
                ls = slice(j * LANES, (j + 1) * LANES)
                m_parts, c_parts, b_parts = [], [], []
                for hh in range(HEADS_PER_SLAB):
                    h = j * HEADS_PER_SLAB + hh
                    col = jnp.broadcast_to(a_cs[:, h:h + 1], (q, q))
                    seg = col - a_cs_t[h:h + 1, :]
                    dec = jnp.exp(jnp.where(causal, seg, NEG_BIG))
                    m_parts.append((cb * dec * dt_t[h:h + 1, :]).astype(BF16))
                    c_parts.append((cg * jnp.exp(col)).astype(BF16))
                    b_parts.append((bg_t * w_t[h:h + 1, :]).astype(BF16))
                xs = act_scr[j, rows, :]
                xs_b = xs.astype(BF16)
                st = state_scr[:, ls]
                st_b = st.astype(BF16)
                zero = jnp.zeros_like(xs_b)
                rhs_x = jnp.concatenate([jnp.where(left, xs_b, zero), jnp.where(left, zero, xs_b)], axis=0)
                rhs_s = jnp.concatenate([jnp.where(left, st_b, zero), jnp.where(left, zero, st_b)], axis=0)
                y = (_dot(jnp.concatenate(m_parts, axis=1), rhs_x)
                     + _dot(jnp.concatenate(c_parts, axis=1), rhs_s)
                     + dfull_ref[:, ls] * xs)
                y_scr[rows, ls] = y
                state_scr[:, ls] = st * cd_full[:, ls] + _dot(jnp.concatenate(b_parts, axis=1), rhs_x)
        return carry

    lax.fori_loop(0, tc // q, chunk, 0, unroll=True)

    v = y_scr[...] * _silu(z_ref[...].astype(F32))
    v = _rms(v, ng_ref[...])
    ys_ref[...] = (_dot(v.astype(BF16), wout_ref[...]) * gs_ref[...].astype(F32)).astype(BF16)


def _ssd(z, xbc, dt, gs, conv_tab, a_log, d_full, expand, norm_g, w_out, batch, seq):
    tc = ROW_TILE
    nst = seq // tc
    row = lambda n: pl.BlockSpec((tc, n), lambda b, s: (b * nst + s, 0))
    return pl.pallas_call(
        _ssd_kernel,
        grid=(batch, nst),
        in_specs=[row(D_MODEL), row(SSD_CONV_CH), row(LANES), row(D_MODEL),
                  _const_spec((XBC_SLABS, SUBLANES, LANES)),
                  _const_spec((1, LANES)), _const_spec((1, D_MODEL)), _const_spec((LANES, D_MODEL)),
                  _const_spec((1, D_MODEL)), _const_spec((D_MODEL, D_MODEL))],
        out_specs=row(D_MODEL),
        out_shape=jax.ShapeDtypeStruct((batch * seq, D_MODEL), BF16),
        scratch_shapes=[pltpu.VMEM((XBC_SLABS, SUBLANES + tc, LANES), F32),
                        pltpu.VMEM((XBC_SLABS, tc, LANES), F32),
                        pltpu.VMEM((SSD_STATE, D_MODEL), F32),
                        pltpu.VMEM((tc, D_MODEL), F32)],
        compiler_params=_params(("arbitrary", "arbitrary")),
        name="ssd",
    )(z, xbc, dt, gs, conv_tab, a_log, d_full, expand, norm_g, w_out)


META_P0, META_P1, META_W0, META_W1 = 0, 1, 2, 3
TAB_ROWS, TAB_START = 0, 1


def _route_and_sort(u, lg, su_ref, meta_ref, tab_ref):
    tm = u.shape[0]
    lane = lax.broadcasted_iota(I32, (tm, LANES), 1)
    far = jnp.int32(4 * LANES)

    def first_argmax(v):
        m = jnp.max(v, axis=-1, keepdims=True)
        return m, jnp.min(jnp.where(v == m, lane, far), axis=-1, keepdims=True)

    is_group = (lane >= N_EXPERTS) & (lane < N_EXPERTS + N_GROUPS)
    gl = jnp.where(is_group, lg, NEG_BIG)
    gmax, glane = first_argmax(gl)
    gidx = glane - N_EXPERTS
    g_prob = 1.0 / jnp.sum(jnp.where(is_group, jnp.exp(gl - gmax), 0.0), axis=-1, keepdims=True)

    in_group = (lane < N_EXPERTS) & ((lane // EXPERTS_PER_GROUP) == gidx)
    el = jnp.where(in_group, lg, NEG_BIG)
    m0, e0 = first_argmax(el)
    m1, e1 = first_argmax(jnp.where(lane == e0, NEG_BIG, el))
    ex = jnp.exp(m1 - m0)
    w0 = g_prob / (1.0 + ex)
    w1 = w0 * ex

    a0 = (lane == e0).astype(F32)
    a1 = (lane == e1).astype(F32)
    c0 = jnp.sum(a0, axis=0, keepdims=True)
    c1 = jnp.sum(a1, axis=0, keepdims=True)
    rows8 = jnp.floor((c0 + c1 + (SUBLANES - 1)) * (1.0 / SUBLANES)) * SUBLANES
    ei = lax.broadcasted_iota(I32, (LANES, LANES), 0)
    ej = lax.broadcasted_iota(I32, (LANES, LANES), 1)
    earlier_expert = (ei < ej).astype(BF16)
    start = _dot(jnp.broadcast_to(rows8, (SUBLANES, LANES)).astype(BF16), earlier_expert)[0:1, :]
    ti = lax.broadcasted_iota(I32, (tm, tm), 0)
    tj = lax.broadcasted_iota(I32, (tm, tm), 1)
    before = (tj < ti).astype(BF16)
    rank = _dot(before, jnp.concatenate([a0, a1], axis=1).astype(BF16))
    p0 = jnp.sum(a0 * (rank[:, :LANES] + start), axis=-1, keepdims=True)
    p1 = jnp.sum(a1 * (rank[:, LANES:] + start + c0), axis=-1, keepdims=True)

    meta = jnp.zeros((tm, LANES), F32)
    for pos, val in ((META_P0, p0), (META_P1, p1), (META_W0, w0), (META_W1, w1)):
        meta = jnp.where(lane == pos, val, meta)
    meta_ref[...] = meta
    sub = lax.broadcasted_iota(I32, (SUBLANES, LANES), 0)
    tab_ref[0] = jnp.where(sub == TAB_ROWS, rows8, jnp.where(sub == TAB_START, start, 0.0))

    meta_t = meta.T
    ri = lax.broadcasted_iota(I32, (LOCAL_ROWS, tm), 0).astype(F32)
    take = (ri == meta_t[META_P0:META_P0 + 1, :]) | (ri == meta_t[META_P1:META_P1 + 1, :])
    gathered = _dot(jnp.where(take, 1.0, 0.0).astype(BF16), u.astype(BF16))
    su_ref[...] = _pack_pairs(gathered, exact_bf16=True)


def _merge_kernel(glu_ref, ys_ref, gc_ref, x_ref, dw_ref, db_ref, lng_ref, lnb_ref, wc_ref, wo_ref,
                  nf_ref, wr_ref, rb_ref, h_ref, su_ref, meta_ref, tab_ref, ext_scr, c_scr):
    tm = x_ref.shape[0]

    @pl.when(pl.program_id(1) == 0)
    def _():
        ext_scr[:, 0:CONF_HALO, :] = jnp.zeros((N_SLABS, CONF_HALO, LANES), F32)

    for j in range(N_SLABS):
        ext_scr[j, CONF_HALO:CONF_HALO + tm, :] = glu_ref[:, j * LANES:(j + 1) * LANES].astype(F32)

    rc = 128
    first = CONF_HALO - (CONF_KERNEL - 1)

    def slab(j, carry):
        for r in range(tm // rc):
            acc = jnp.zeros((rc, LANES), F32)
            for k in range(CONF_KERNEL):
                acc = acc + dw_ref[j, k:k + 1, :] * ext_scr[j, pl.ds(r * rc + first + k, rc), :]
            c_scr[j, r * rc:(r + 1) * rc, :] = acc
        return carry

    lax.fori_loop(0, N_SLABS, slab, 0, unroll=2)
    for j in range(N_SLABS):
        ext_scr[j, 0:CONF_HALO, :] = ext_scr[j, tm:tm + CONF_HALO, :]

    c = jnp.concatenate([c_scr[j] for j in range(N_SLABS)], axis=1) + db_ref[...]
    mu = jnp.mean(c, axis=-1, keepdims=True)
    xc = c - mu
    yln = xc * lax.rsqrt(jnp.mean(xc * xc, axis=-1, keepdims=True) + EPS) * lng_ref[...] + lnb_ref[...]
    y_conv = _dot(_silu(yln).astype(BF16), wc_ref[...])
    mix = gc_ref[...].astype(F32) * y_conv + ys_ref[...].astype(F32)
    h = x_ref[...] + _dot(mix.astype(BF16), wo_ref[...])
    h_ref[...] = h
    u = _rms(h, nf_ref[...])
    u_hi, u_lo = _split_bf16(u, 2)
    both = _dot(u_hi, wr_ref[...])
    lg = both[:, :LANES] + both[:, LANES:] + _dot(u_lo, wr_ref[:, :LANES]) + rb_ref[...]
    _route_and_sort(u, lg, su_ref, meta_ref, tab_ref)


def _merge(glu, ys, gc, x2, dw3, dw_b, ln_g, ln_b, w_conf, w_o, nf_g, wr_cat, rb, batch, seq):
    tm = ROW_TILE
    nst = seq // tm
    t = batch * seq
    ntile = t // tm
    tile = lambda b, s: b * nst + s
    row = lambda n: pl.BlockSpec((tm, n), lambda b, s: (tile(b, s), 0))
    return pl.pallas_call(
        _merge_kernel,
        grid=(batch, nst),
        in_specs=[row(D_MODEL), row(D_MODEL), row(D_MODEL), row(D_MODEL),
                  _const_spec((N_SLABS, CONF_HALO, LANES)), _const_spec((1, D_MODEL)),
                  _const_spec((1, D_MODEL)), _const_spec((1, D_MODEL)),
                  _const_spec((D_MODEL, D_MODEL)), _const_spec((D_MODEL, D_MODEL)),
                  _const_spec((1, D_MODEL)), _const_spec((D_MODEL, 2 * LANES)), _const_spec((1, LANES))],
        out_specs=[row(D_MODEL),
                   pl.BlockSpec((LOCAL_ROWS, HALF), lambda b, s: (tile(b, s), 0)),
                   row(LANES),
                   pl.BlockSpec((1, SUBLANES, LANES), lambda b, s: (tile(b, s), 0, 0))],
        out_shape=[jax.ShapeDtypeStruct((t, D_MODEL), F32),
                   jax.ShapeDtypeStruct((ntile * LOCAL_ROWS, HALF), I32),
                   jax.ShapeDtypeStruct((t, LANES), F32),
                   jax.ShapeDtypeStruct((ntile, SUBLANES, LANES), F32)],
        scratch_shapes=[pltpu.VMEM((N_SLABS, CONF_HALO + tm, LANES), F32),
                        pltpu.VMEM((N_SLABS, tm, LANES), F32)],
        compiler_params=_params(("arbitrary", "arbitrary")),
        name="merge",
    )(glu, ys, gc, x2, dw3, dw_b, ln_g, ln_b, w_conf, w_o, nf_g, wr_cat, rb)


BIG_PIECE = MOE_BLOCK_ROWS
REGROUP_TILES = 4


LONG_COPY_ROWS = 4 * SUBLANES
LONG_COPY_PRIORITY, SHORT_COPY_PRIORITY = 0, 1


def _piece_loops(n_pieces_of, copy_of):
    per_long = LONG_COPY_ROWS // SUBLANES

    def run(action):
        def chunk(e, c):
            n = n_pieces_of(e)
            n_long = lax.shift_right_logical(n, per_long.bit_length() - 1)

            def long_copy(k, cc):
                action(copy_of(e, k * LONG_COPY_ROWS, LONG_COPY_ROWS), LONG_COPY_PRIORITY)
                return cc

            def short_copy(k, cc):
                action(copy_of(e, n_long * LONG_COPY_ROWS + k * SUBLANES, SUBLANES), SHORT_COPY_PRIORITY)
                return cc

            lax.fori_loop(0, n_long, long_copy, 0)
            lax.fori_loop(0, n & (per_long - 1), short_copy, 0)
            return c
        lax.fori_loop(0, N_EXPERTS, chunk, 0)
    return run


WAIT_SIZES = tuple(SUBLANES << b for b in range((LOCAL_ROWS // SUBLANES).bit_length() - 1, -1, -1))


def _wait_rows(n_rows, copy_of_rows):
    for size in WAIT_SIZES:
        @pl.when((n_rows & size) != 0)
        def _(size=size):
            copy_of_rows(size).wait()


def _regroup_kernel(start, dst_off, npc, used, zdst, znpc, bdst, bnpc, su_ref, xb_ref, zero_scr, sem):
    i = pl.program_id(0)

    def rows(o, n):
        return pl.ds(pl.multiple_of(o, SUBLANES), n)

    @pl.when(i == 0)
    def _():
        zero_scr[...] = jnp.zeros_like(zero_scr)
        zero_pieces = _piece_loops(
            lambda e: znpc[e],
            lambda e, off, n: pltpu.make_async_copy(zero_scr.at[pl.ds(0, n)],
                                                    xb_ref.at[rows(zdst[e] + off, n)], sem))
        zero_pieces(lambda cp, priority: cp.start(priority=priority))

        def big(k):
            return pltpu.make_async_copy(zero_scr, xb_ref.at[rows(bdst[0] + k * BIG_PIECE, BIG_PIECE)], sem)

        def big_start(k, c):
            big(k).start()
            return c

        def big_wait(k, c):
            big(k).wait()
            return c

        lax.fori_loop(0, bnpc[0], big_start, 0)
        zero_pieces(lambda cp, priority: cp.wait())
        lax.fori_loop(0, bnpc[0], big_wait, 0)

    for sub in range(REGROUP_TILES):
        tile = i * REGROUP_TILES + sub
        base = tile * N_EXPERTS
        local = sub * LOCAL_ROWS
        data_pieces = _piece_loops(
            lambda e: npc[base + e],
            lambda e, off, n: pltpu.make_async_copy(su_ref.at[rows(local + start[base + e] + off, n)],
                                                    xb_ref.at[rows(dst_off[base + e] + off, n)], sem))
        data_pieces(lambda cp, priority: cp.start(priority=priority))
    for sub in range(REGROUP_TILES):
        _wait_rows(used[i * REGROUP_TILES + sub],
                   lambda n: pltpu.make_async_copy(su_ref.at[pl.ds(0, n)], xb_ref.at[pl.ds(0, n)], sem))


def _regroup(start, dst_off, npc, used, zdst, znpc, bdst, bnpc, su, out_rows):
    ntile = su.shape[0] // LOCAL_ROWS
    assert ntile % REGROUP_TILES == 0
    return pl.pallas_call(
        _regroup_kernel,
        grid_spec=pltpu.PrefetchScalarGridSpec(
            num_scalar_prefetch=8, grid=(ntile // REGROUP_TILES,),
            in_specs=[pl.BlockSpec((REGROUP_TILES * LOCAL_ROWS, HALF), lambda i, *_: (i, 0))],
            out_specs=pl.BlockSpec(memory_space=pl.ANY),
            scratch_shapes=[pltpu.VMEM((BIG_PIECE, HALF), I32), pltpu.SemaphoreType.DMA(())]),
        out_shape=jax.ShapeDtypeStruct((out_rows, HALF), I32),
        compiler_params=_params(("arbitrary",)),
        name="regroup",
    )(start, dst_off, npc, used, zdst, znpc, bdst, bnpc, su)


def _ffn_kernel(be_ref, nv_ref, x_ref, wg_ref, wu_ref, wd_ref, y_ref):
    del be_ref

    @pl.when(pl.program_id(0) < nv_ref[0])
    def _():
        x_lo, x_hi = _unpack_pairs(x_ref[...])
        wg = wg_ref[0].astype(BF16)
        wu = wu_ref[0].astype(BF16)
        a = _dot(x_lo, wg[:HALF]) + _dot(x_hi, wg[HALF:])
        b = _dot(x_lo, wu[:HALF]) + _dot(x_hi, wu[HALF:])
        y_ref[...] = _pack_pairs(_dot((_silu(a) * b).astype(BF16), wd_ref[0].astype(BF16)))

    @pl.when(pl.program_id(0) >= nv_ref[0])
    def _():
        y_ref[...] = jnp.zeros_like(y_ref)


def _ffn(blk_e, n_valid, xb, wg, wu, wd):
    rows = xb.shape[0]
    bk = MOE_BLOCK_ROWS
    xmap = lambda i, be, nv: (jnp.minimum(i, nv[0] - 1), 0)
    ymap = lambda i, be, nv: (i, 0)
    wmap = lambda i, be, nv: (be[i], 0, 0)
    return pl.pallas_call(
        _ffn_kernel,
        grid_spec=pltpu.PrefetchScalarGridSpec(
            num_scalar_prefetch=2,
            grid=(rows // bk,),
            in_specs=[pl.BlockSpec((bk, HALF), xmap),
                      pl.BlockSpec((1, D_MODEL, D_EXPERT), wmap),
                      pl.BlockSpec((1, D_MODEL, D_EXPERT), wmap),
                      pl.BlockSpec((1, D_EXPERT, D_MODEL), wmap)],
            out_specs=pl.BlockSpec((bk, HALF), ymap)),
        out_shape=jax.ShapeDtypeStruct((rows, HALF), I32),
        compiler_params=_params(("arbitrary",)),
        name="expert_ffn",
    )(blk_e, n_valid, xb, wg, wu, wd)


def _combine_kernel(start, src_off, npc, used, meta_ref, h_ref, p_ref, yb_ref, gp_ref, wpg_ref, wpp_ref,
                    gf_ref, o_ref, yl_ref, sel_scr, ple_scr, sem):
    tm = h_ref.shape[0]
    i = pl.program_id(0)
    piece = SUBLANES

    def rows(o, n):
        return pl.ds(pl.multiple_of(o, piece), n)

    base = i * N_EXPERTS
    pieces = _piece_loops(
        lambda e: npc[base + e],
        lambda e, off, n: pltpu.make_async_copy(yb_ref.at[rows(src_off[base + e] + off, n)],
                                                yl_ref.at[rows(start[base + e] + off, n)], sem))
    pieces(lambda cp, priority: cp.start(priority=priority))

    def zero_tail(k, c):
        yl_ref[rows(used[i] + k * piece, piece), :] = jnp.zeros((piece, HALF), I32)
        return c

    lax.fori_loop(0, (LOCAL_ROWS - used[i]) // piece, zero_tail, 0)

    m = meta_ref[...]
    ci = lax.broadcasted_iota(I32, (tm, LOCAL_ROWS), 1).astype(F32)
    sel_scr[...] = (jnp.where(ci == m[:, META_P0:META_P0 + 1], m[:, META_W0:META_W0 + 1], 0.0)
                    + jnp.where(ci == m[:, META_P1:META_P1 + 1], m[:, META_W1:META_W1 + 1], 0.0)
                    ).astype(BF16)
    ple_scr[...] = _dot(p_ref[...].astype(BF16), wpp_ref[...])
    _wait_rows(used[i], lambda n: pltpu.make_async_copy(yb_ref.at[pl.ds(0, n)], yl_ref.at[pl.ds(0, n)], sem))

    y_lo, y_hi = _unpack_pairs(yl_ref[...])
    moe = jnp.concatenate([_dot(sel_scr[...], y_lo), _dot(sel_scr[...], y_hi)], axis=1)
    h = h_ref[...] + moe
    gate = _sigmoid(_dot(_rms(h, gp_ref[...]).astype(BF16), wpg_ref[...]))
    h = h + ple_scr[...] * gate
    o_ref[...] = _rms(h, gf_ref[...])


def _combine(start, src_off, npc, used, meta, h1, p2, yb, g_ple, w_pg, w_pp, g_fin):
    t = h1.shape[0]
    tm = ROW_TILE
    row = lambda n: pl.BlockSpec((tm, n), lambda i, *_: (i, 0))
    return pl.pallas_call(
        _combine_kernel,
        grid_spec=pltpu.PrefetchScalarGridSpec(
            num_scalar_prefetch=4, grid=(t // tm,),
            in_specs=[row(LANES), row(D_MODEL), row(PLE_DIM), pl.BlockSpec(memory_space=pl.ANY),
                      _const_spec((1, D_MODEL)), _const_spec((D_MODEL, D_MODEL)),
                      _const_spec((PLE_DIM, D_MODEL)), _const_spec((1, D_MODEL))],
            out_specs=row(D_MODEL),
            scratch_shapes=[pltpu.VMEM((LOCAL_ROWS, HALF), I32), pltpu.VMEM((tm, LOCAL_ROWS), BF16),
                            pltpu.VMEM((tm, D_MODEL), F32), pltpu.SemaphoreType.DMA(())]),
        out_shape=jax.ShapeDtypeStruct((t, D_MODEL), F32),
        compiler_params=_params(("arbitrary",)),
        name="combine",
    )(start, src_off, npc, used, meta, h1, p2, yb, g_ple, w_pg, w_pp, g_fin)


def _row(v, n=None):
    v = v.astype(F32).reshape(1, -1)
    if n is not None and v.shape[1] < n:
        v = jnp.pad(v, ((0, 0), (0, n - v.shape[1])))
    return v


def _layer(h2, p2, batch, seq, norm_mix_g, w_in, ssd_conv_w, ssd_conv_b, ssd_dt_bias, ssd_a_log, ssd_d,
           ssd_norm_g, w_ssd_out, conf_dw_w, conf_dw_b, conf_ln_g, conf_ln_b, w_conf_out, w_o,
           norm_ffn_g, router_group_w, router_group_b, router_expert_w, router_expert_b,
           expert_w_gate, expert_w_up, expert_w_down, norm_ple_g, w_ple_gate, w_ple_proj, final_g):
    t = batch * seq
    tm = ROW_TILE
    ntile = t // tm
    w_main = jnp.concatenate([w_in[:, :OFF_DT], w_in[:, OFF_GLU:]], axis=1).astype(BF16)
    w_dt = jnp.pad(w_in[:, OFF_DT:OFF_GLU], ((0, 0), (0, LANES - SSD_HEADS))).astype(BF16)
    conv_tab = jnp.concatenate([ssd_conv_w.astype(F32), ssd_conv_b.astype(F32)[None, :]], axis=0)
    conv_tab = jnp.pad(conv_tab, ((0, SUBLANES - conv_tab.shape[0]), (0, 0)))
    conv_tab = conv_tab.reshape(SUBLANES, XBC_SLABS, LANES).transpose(1, 0, 2)
    d_full = jnp.repeat(ssd_d.astype(F32), SSD_HEADDIM).reshape(1, D_MODEL)
    head_of_lane = jnp.arange(D_MODEL) // SSD_HEADDIM
    expand = (jnp.arange(LANES)[:, None] == head_of_lane[None, :]).astype(BF16)
    dw3 = jnp.pad(conf_dw_w.astype(F32), ((0, CONF_HALO - CONF_KERNEL), (0, 0)))
    dw3 = dw3.reshape(CONF_HALO, N_SLABS, LANES).transpose(1, 0, 2)
    wr = jnp.concatenate([router_expert_w, router_group_w], axis=1).astype(F32)
    wr = jnp.pad(wr, ((0, 0), (0, LANES - wr.shape[1])))
    wr_hi = wr.astype(BF16)
    wr_cat = jnp.concatenate([wr_hi, (wr - wr_hi.astype(F32)).astype(BF16)], axis=1)
    rb = _row(jnp.concatenate([router_expert_b, router_group_b]), LANES)

    z, xbc, dt, glu, gs, gc = _inproj(h2, _row(norm_mix_g), w_main, w_dt, _row(ssd_dt_bias, LANES))
    ys = _ssd(z, xbc, dt, gs, conv_tab, _row(ssd_a_log, LANES), d_full, expand,
              _row(ssd_norm_g), w_ssd_out.astype(BF16), batch, seq)
    h1, su, meta, tab = _merge(glu, ys, gc, h2, dw3, _row(conf_dw_b), _row(conf_ln_g), _row(conf_ln_b),
                               w_conf_out.astype(BF16), w_o.astype(BF16), _row(norm_ffn_g), wr_cat, rb,
                               batch, seq)

    bk = MOE_BLOCK_ROWS
    n_blocks = (2 * t + ntile * N_EXPERTS * (SUBLANES - 1)) // bk + N_EXPERTS + 1
    total_rows = n_blocks * bk
    rows8 = tab[:, TAB_ROWS, :N_EXPERTS].astype(I32)
    start = tab[:, TAB_START, :N_EXPERTS].astype(I32)
    per_expert = jnp.sum(rows8, axis=0)
    region = (per_expert + bk - 1) // bk * bk
    region_end = jnp.cumsum(region)
    region_start = region_end - region
    n_valid = (region_end[-1] // bk).astype(I32)
    global_off = region_start[None, :] + jnp.cumsum(rows8, axis=0) - rows8
    npc = (rows8 // SUBLANES).reshape(-1)
    start = start.reshape(-1)
    global_off = global_off.reshape(-1).astype(I32)
    blk = jnp.arange(n_blocks, dtype=I32)
    blk_e = jnp.sum((blk[:, None] * bk >= region_end[None, :]).astype(I32), axis=1)
    last_e = jnp.sum((((n_valid - 1) * bk) >= region_end).astype(I32))
    blk_e = jnp.minimum(jnp.where(blk < n_valid, blk_e, last_e), N_EXPERTS - 1).astype(I32)
    one = lambda v: jnp.reshape(v, (1,)).astype(I32)

    used = jnp.sum(rows8, axis=1).astype(I32)
    xb = _regroup(start, global_off, npc, used,
                  (region_start + per_expert).astype(I32), ((region - per_expert) // SUBLANES).astype(I32),
                  one(region_end[-1]), one(n_blocks - n_valid), su, total_rows)
    yb = _ffn(blk_e, one(n_valid), xb, expert_w_gate, expert_w_up, expert_w_down)
    return _combine(start, global_off, npc, used, meta, h1, p2, yb, _row(norm_ple_g),
                    w_ple_gate.astype(BF16), w_ple_proj.astype(BF16), final_g)


def kernel(x, p, norm_mix_g, w_in, ssd_conv_w, ssd_conv_b, ssd_dt_bias, ssd_a_log, ssd_d, ssd_norm_g,
           w_ssd_out, conf_dw_w, conf_dw_b, conf_ln_g, conf_ln_b, w_conf_out, w_o, norm_ffn_g,
           router_group_w, router_group_b, router_expert_w, router_expert_b, expert_w_gate, expert_w_up,
           expert_w_down, norm_ple_g, w_ple_gate, w_ple_proj, final_norm_g):
    batch, seq, d = x.shape
    depth = p.shape[0]
    assert d == D_MODEL and depth == 1, "single-layer block with D_MODEL features"
    assert seq % ROW_TILE == 0 and ROW_TILE % SSD_CHUNK == 0
    h2 = x.reshape(batch * seq, d)
    out = _layer(h2, p[0].reshape(batch * seq, PLE_DIM), batch, seq, norm_mix_g[0], w_in[0], ssd_conv_w[0],
                 ssd_conv_b[0], ssd_dt_bias[0], ssd_a_log[0], ssd_d[0], ssd_norm_g[0], w_ssd_out[0],
                 conf_dw_w[0], conf_dw_b[0], conf_ln_g[0], conf_ln_b[0], w_conf_out[0], w_o[0],
                 norm_ffn_g[0], router_group_w[0], router_group_b[0], router_expert_w[0],
                 router_expert_b[0], expert_w_gate[0], expert_w_up[0], expert_w_down[0], norm_ple_g[0],
                 w_ple_gate[0], w_ple_proj[0], _row(final_norm_g))
    return out.reshape(batch, seq, d)
```
